```python
import math
import jax, jax.numpy as jnp
from jax import lax
import numpy as np

D_MODEL = 1024
BATCH = 32
SEQ = 256
DEPTH = 2
DEC_BATCH = 2
DEC_SEQ = 1024
PAST_LEN = 256

GRID_W = 64
N_MIXERS = 2
N_GLA = (DEPTH + 1) // 2
N_ATTN = DEPTH // 2
GLA_HEADS = 4
GLA_DK = D_MODEL // 2 // GLA_HEADS
GLA_DV = D_MODEL // GLA_HEADS
GLA_KD = GLA_HEADS * GLA_DK
GLA_VD = GLA_HEADS * GLA_DV
GLA_GATE_RANK = 16
GLA_TAU = 16.0
GLA_CHUNK = 64
ATTN_HEADS = 8
ATTN_KV_HEADS = 2
HEAD_DIM = D_MODEL // ATTN_HEADS
ATTN_QD = ATTN_HEADS * HEAD_DIM
ATTN_KVD = ATTN_KV_HEADS * HEAD_DIM
ROPE_AXIS_DIM = HEAD_DIM // 2
ROPE_THETA = 10000.0
Q_BLOCK = 128
N_EXPERTS = 16
EC_CAPACITY_FACTOR = 2
D_FF_EXPERT = 2 * D_MODEL
ALPHA = (2 * DEPTH) ** 0.25
BETA = (8 * DEPTH) ** -0.25
NORM_EPS = 1e-6

kernel_name = "hybrid_gla_gqa_ec_moe_diffusion_step"


def layer_norm(x, g, b):
    xf = x.astype(jnp.float32)
    mu = jnp.mean(xf, axis=-1, keepdims=True)
    var = jnp.mean(jnp.square(xf - mu), axis=-1, keepdims=True)
    return ((xf - mu) * lax.rsqrt(var + NORM_EPS) * g + b).astype(x.dtype)


def rms_norm(x, g):
    xf = x.astype(jnp.float32)
    return (xf * lax.rsqrt(jnp.mean(jnp.square(xf), axis=-1, keepdims=True) + NORM_EPS) * g).astype(x.dtype)


def adaln_params(cvec, w_mod, b_mod):
    m = (jax.nn.silu(cvec) @ w_mod + b_mod)[:, None, :]
    return jnp.split(m, 6, axis=-1)


def split_heads(x, n):
    B, T, _ = x.shape
    return x.reshape(B, T, n, -1).transpose(0, 2, 1, 3)


def gla_chunk_scan(q, k, v, logg, S0):
    B, H, T, DK = q.shape
    DV = v.shape[-1]
    n = T // GLA_CHUNK
    rs = lambda a: a.reshape(B, H, n, GLA_CHUNK, a.shape[-1])
    q, k, v, logg = rs(q), rs(k), rs(v), rs(logg)
    G = jnp.cumsum(logg, axis=3)
    G_last = G[:, :, :, -1:, :]
    qg = q * jnp.exp(G)
    kg = k * jnp.exp(-G)
    mask = jnp.tril(jnp.ones((GLA_CHUNK, GLA_CHUNK), dtype=bool))
    A = jnp.where(mask, jnp.einsum('bhncd,bhnsd->bhncs', qg, kg), 0.0)
    o_intra = jnp.einsum('bhncs,bhnse->bhnce', A, v)
    dS = jnp.einsum('bhncd,bhnce->bhnde', k * jnp.exp(G_last - G), v)
    decay = jnp.exp(G_last[:, :, :, 0, :])

    def step(S, xs):
        qg_c, decay_c, dS_c = xs
        o_c = jnp.einsum('bhcd,bhde->bhce', qg_c, S)
        return decay_c[..., None] * S + dS_c, o_c

    S_fin, o_inter = lax.scan(step, S0.astype(jnp.float32),
                              (jnp.moveaxis(qg, 2, 0), jnp.moveaxis(decay, 2, 0), jnp.moveaxis(dS, 2, 0)))
    o = o_intra + jnp.moveaxis(o_inter, 0, 2)
    return o.reshape(B, H, T, DV), S_fin


def gla_mixer(h, S0_f, S0_b, w_in, w_gf1, w_gf2, b_gf, w_gb1, w_gb2, b_gb, g_norm, w_out):
    B, T, _ = h.shape
    proj = h @ w_in
    q, k, v, r = jnp.split(proj, [GLA_KD, 2 * GLA_KD, 2 * GLA_KD + GLA_VD], axis=-1)
    f32 = lambda a, n: split_heads(a, n).astype(jnp.float32)
    q = f32(q, GLA_HEADS) * GLA_DK ** -0.5
    k = f32(k, GLA_HEADS)
    v = f32(v, GLA_HEADS)

    def log_gate(w1, w2, b):
        z = ((h @ w1) @ w2 + b).astype(jnp.float32)
        return split_heads(jax.nn.log_sigmoid(z) / GLA_TAU, GLA_HEADS)

    flip = lambda a: a[:, :, ::-1]
    o_f, S_f = gla_chunk_scan(q, k, v, log_gate(w_gf1, w_gf2, b_gf), S0_f)
    o_b, S_b = gla_chunk_scan(flip(q), flip(k), flip(v), flip(log_gate(w_gb1, w_gb2, b_gb)), S0_b)
    o = (o_f + flip(o_b)).transpose(0, 2, 1, 3)
    o = rms_norm(o, g_norm.reshape(GLA_HEADS, GLA_DV))
    o = o.reshape(B, T, GLA_VD).astype(h.dtype) * jax.nn.silu(r)
    return o @ w_out, S_f, S_b


def axial_rope_tables(T):
    rows = T // GRID_W
    row = jnp.repeat(jnp.arange(rows), GRID_W).astype(jnp.float32)
    col = jnp.tile(jnp.arange(GRID_W), rows).astype(jnp.float32)
    inv = ROPE_THETA ** (-jnp.arange(0, ROPE_AXIS_DIM, 2, dtype=jnp.float32) / ROPE_AXIS_DIM)
    ang = jnp.stack([row[:, None] * inv, col[:, None] * inv], axis=1)
    return jnp.cos(ang), jnp.sin(ang)


def apply_axial_rope(x, cos, sin):
    B, H, T, _ = x.shape
    xs = x.astype(jnp.float32).reshape(B, H, T, 2, 2, ROPE_AXIS_DIM // 2)
    x1, x2 = xs[..., 0, :], xs[..., 1, :]
    out = jnp.stack([x1 * cos - x2 * sin, x2 * cos + x1 * sin], axis=-2)
    return out.reshape(B, H, T, HEAD_DIM).astype(x.dtype)


def block_attention(q, k, v):
    B, Hq, T, hd = q.shape
    Hkv = k.shape[1]
    G = Hq // Hkv
    nb = T // Q_BLOCK
    qb = q.reshape(B, Hkv, G, nb, Q_BLOCK, hd).transpose(3, 0, 1, 2, 4, 5)

    def one_block(qblk):
        s = jnp.einsum('bkgqd,bksd->bkgqs', qblk, k).astype(jnp.float32) * hd ** -0.5
        p = jax.nn.softmax(s, axis=-1).astype(v.dtype)
        return jnp.einsum('bkgqs,bksd->bkgqd', p, v)

    o = lax.map(one_block, qb)
    return o.transpose(1, 2, 3, 0, 4, 5).reshape(B, Hq, T, hd)


def attn_mixer(h, ctx_k, ctx_v, w_in, g_q, g_k, w_out):
    B, T, _ = h.shape
    q, k, v = jnp.split(h @ w_in, [ATTN_QD, ATTN_QD + ATTN_KVD], axis=-1)
    q = rms_norm(split_heads(q, ATTN_HEADS), g_q)
    k = rms_norm(split_heads(k, ATTN_KV_HEADS), g_k)
    v = split_heads(v, ATTN_KV_HEADS)
    if ctx_k is None:
        keys, vals = k, v
    else:
        cos, sin = axial_rope_tables(T)
        q = apply_axial_rope(q, cos, sin)
        k = apply_axial_rope(k, cos, sin)
        keys = jnp.concatenate([ctx_k.astype(k.dtype), k], axis=2)
        vals = jnp.concatenate([ctx_v.astype(v.dtype), v], axis=2)
    o = block_attention(q, keys, vals).transpose(0, 2, 1, 3).reshape(B, T, ATTN_QD)
    return o @ w_out, k, v


def expert_choice_ffn(h, w_router, w_gate, w_up, w_down):
    B, T, D = h.shape
    N = B * T
    C = EC_CAPACITY_FACTOR * N // N_EXPERTS
    xt = h.reshape(N, D)
    aff = jax.nn.softmax((xt @ w_router).astype(jnp.float32), axis=-1)
    g, idx = lax.top_k(aff.T, C)
    xe = xt[idx]
    hid = jax.nn.silu(jnp.einsum('ecd,edf->ecf', xe, w_gate)) * jnp.einsum('ecd,edf->ecf', xe, w_up)
    ye = jnp.einsum('ecf,efd->ecd', hid, w_down) * g[..., None].astype(h.dtype)
    out = jnp.zeros_like(xt).at[idx.reshape(-1)].add(ye.reshape(-1, D).astype(xt.dtype))
    return out.reshape(B, T, D)


def setup_inputs(seed: int = 0) -> dict:
    key = jax.random.key(seed)
    ks = iter(jax.random.split(key, 40))
    nrm = lambda shape, s: jax.random.normal(next(ks), shape, jnp.float32) * s
    D = D_MODEL
    return {
        "x_prompt": nrm((BATCH, SEQ, D), 1.0),
        "x_sample": nrm((DEC_BATCH, DEC_SEQ, D), 1.0),
        "state_gla_fwd": nrm((DEC_BATCH, N_GLA, GLA_HEADS, GLA_DK, GLA_DV), 0.5),
        "state_gla_bwd": nrm((DEC_BATCH, N_GLA, GLA_HEADS, GLA_DK, GLA_DV), 0.5),
        "cache_attn_k": nrm((DEC_BATCH, N_ATTN, ATTN_KV_HEADS, PAST_LEN, HEAD_DIM), 1.0),
        "cache_attn_v": nrm((DEC_BATCH, N_ATTN, ATTN_KV_HEADS, PAST_LEN, HEAD_DIM), 1.0),
        "c": nrm((DEC_BATCH, D), 1.0),
        "c_ctx": nrm((D,), 1.0),
        "w_mod": nrm((DEPTH, D, 6 * D), 0.5 * D ** -0.5),
        "b_mod": nrm((DEPTH, 6 * D), 0.02),
        "ln_g": 1.0 + nrm((DEPTH, 2, D), 0.02),
        "ln_b": nrm((DEPTH, 2, D), 0.02),
        "w_gla_in": nrm((N_GLA, D, 2 * GLA_KD + 2 * GLA_VD), D ** -0.5),
        "w_gla_gf1": nrm((N_GLA, D, GLA_GATE_RANK), D ** -0.5),
        "w_gla_gf2": nrm((N_GLA, GLA_GATE_RANK, GLA_KD), GLA_GATE_RANK ** -0.5),
        "b_gla_gf": nrm((N_GLA, GLA_KD), 0.1),
        "w_gla_gb1": nrm((N_GLA, D, GLA_GATE_RANK), D ** -0.5),
        "w_gla_gb2": nrm((N_GLA, GLA_GATE_RANK, GLA_KD), GLA_GATE_RANK ** -0.5),
        "b_gla_gb": nrm((N_GLA, GLA_KD), 0.1),
        "g_gla_norm": 1.0 + nrm((N_GLA, GLA_VD), 0.02),
        "w_gla_out": nrm((N_GLA, GLA_VD, D), BETA * GLA_VD ** -0.5),
        "w_attn_in": nrm((N_ATTN, D, ATTN_QD + 2 * ATTN_KVD), D ** -0.5),
        "g_attn_q": 1.0 + nrm((N_ATTN, HEAD_DIM), 0.02),
        "g_attn_k": 1.0 + nrm((N_ATTN, HEAD_DIM), 0.02),
        "w_attn_out": nrm((N_ATTN, ATTN_QD, D), BETA * ATTN_QD ** -0.5),
        "w_router": nrm((DEPTH, D, N_EXPERTS), D ** -0.5),
        "w_moe_gate": nrm((DEPTH, N_EXPERTS, D, D_FF_EXPERT), D ** -0.5),
        "w_moe_up": nrm((DEPTH, N_EXPERTS, D, D_FF_EXPERT), D ** -0.5),
        "w_moe_down": nrm((DEPTH, N_EXPERTS, D_FF_EXPERT, D), BETA * D_FF_EXPERT ** -0.5),
    }


def reference(x_prompt, x_sample, state_gla_fwd, state_gla_bwd, cache_attn_k, cache_attn_v, c, c_ctx,
              w_mod, b_mod, ln_g, ln_b,
              w_gla_in, w_gla_gf1, w_gla_gf2, b_gla_gf, w_gla_gb1, w_gla_gb2, b_gla_gb, g_gla_norm, w_gla_out,
              w_attn_in, g_attn_q, g_attn_k, w_attn_out,
              w_router, w_moe_gate, w_moe_up, w_moe_down):
    xp, xs = x_prompt, x_sample
    Bp = xp.shape[0]
    new_f, new_b, new_k, new_v = [], [], [], []
    for i in range(DEPTH):
        j = i // N_MIXERS
        sh1_p, sc1_p, ga1_p, sh2_p, sc2_p, ga2_p = adaln_params(c_ctx[None, :], w_mod[i], b_mod[i])
        sh1_s, sc1_s, ga1_s, sh2_s, sc2_s, ga2_s = adaln_params(c, w_mod[i], b_mod[i])
        hp = xp * (1 + sc1_p) + sh1_p
        hs = xs * (1 + sc1_s) + sh1_s
        if i % N_MIXERS == 0:
            gla_w = (w_gla_in[j], w_gla_gf1[j], w_gla_gf2[j], b_gla_gf[j], w_gla_gb1[j], w_gla_gb2[j],
                     b_gla_gb[j], g_gla_norm[j], w_gla_out[j])
            zero_state = jnp.zeros((Bp, GLA_HEADS, GLA_DK, GLA_DV), jnp.float32)
            mp, s_f, s_b = gla_mixer(hp, zero_state, zero_state, *gla_w)
            ms, _, _ = gla_mixer(hs, state_gla_fwd[:, j], state_gla_bwd[:, j], *gla_w)
            new_f.append(s_f)
            new_b.append(s_b)
        else:
            attn_w = (w_attn_in[j], g_attn_q[j], g_attn_k[j], w_attn_out[j])
            mp, k_ctx, v_ctx = attn_mixer(hp, None, None, *attn_w)
            ms, _, _ = attn_mixer(hs, cache_attn_k[:, j], cache_attn_v[:, j], *attn_w)
            new_k.append(k_ctx)
            new_v.append(v_ctx)
        xp = layer_norm(ALPHA * xp + ga1_p * mp, ln_g[i, 0], ln_b[i, 0])
        xs = layer_norm(ALPHA * xs + ga1_s * ms, ln_g[i, 0], ln_b[i, 0])
        moe_w = (w_router[i], w_moe_gate[i], w_moe_up[i], w_moe_down[i])
        fp = expert_choice_ffn(xp * (1 + sc2_p) + sh2_p, *moe_w)
        fs = expert_choice_ffn(xs * (1 + sc2_s) + sh2_s, *moe_w)
        xp = layer_norm(ALPHA * xp + ga2_p * fp, ln_g[i, 1], ln_b[i, 1])
        xs = layer_norm(ALPHA * xs + ga2_s * fs, ln_g[i, 1], ln_b[i, 1])
    return (xp, xs, jnp.stack(new_f, axis=1), jnp.stack(new_b, axis=1),
            jnp.stack(new_k, axis=1), jnp.stack(new_v, axis=1))
```

```python
import functools
import math

import jax
import jax.numpy as jnp
from jax import lax
from jax.experimental import pallas as pl
from jax.experimental.pallas import tpu as pltpu

F32 = jnp.float32
BF16 = jnp.bfloat16
I32 = jnp.int32

D = 1024
BATCH, SEQ = 32, 256
DEC_BATCH, DEC_SEQ = 2, 1024
PAST_LEN = 256
DEPTH = 2
GRID_W = 64
NP = BATCH * SEQ
NS = DEC_BATCH * DEC_SEQ
NT = NP + NS
NMOD = 1 + DEC_BATCH
GLA_HEADS, GLA_DK, GLA_DV = 4, 128, 256
GLA_KD, GLA_VD = 512, 1024
GLA_RANK = 16
GLA_TAU = 16.0
CHUNK = 64
ATTN_HEADS, ATTN_KV, HEAD_DIM = 8, 2, 128
ATTN_G = ATTN_HEADS // ATTN_KV
ATTN_QD, ATTN_KVD = 1024, 256
ROPE_AXIS_DIM = 64
ROPE_THETA = 10000.0
E = 16
CP = 2 * NP // E
CS = 2 * NS // E
CT = CP + CS
FF = 2 * D
ALPHA = (2 * DEPTH) ** 0.25
EPS = 1e-6

LANES = 128
VMEM_LIMIT = 56 * 1024 * 1024
TM = 512
GLA_BLK = 1024
TF = 512
TC = 256
RK = 256
ZROWS = 2 * NT
NBLK = 128


def _cparams(sem):
    return pltpu.CompilerParams(dimension_semantics=sem, vmem_limit_bytes=VMEM_LIMIT)


def _bf(x):
    return x.astype(BF16)


def _dot(a, b):
    return jnp.dot(a, b, preferred_element_type=F32)


def _dot_nt(a, b):
    return lax.dot_general(a, b, (((1,), (1,)), ((), ())), preferred_element_type=F32)


def _dot_tn(a, b):
    return lax.dot_general(a, b, (((0,), (0,)), ((), ())), preferred_element_type=F32)


def _layer_norm(y, g, b):
    mu = jnp.mean(y, axis=-1, keepdims=True)
    yc = y - mu
    var = jnp.mean(yc * yc, axis=-1, keepdims=True)
    return yc * lax.rsqrt(var + EPS) * g + b


def _silu(x):
    return x * jax.nn.sigmoid(x)


def _mod_row(i):
    return jnp.where(i < NP // TM, 0, 1 + (i - NP // TM) // (DEC_SEQ // TM))


ADA_TN = 1024


def _adaln_kernel(c_ref, w_ref, b_ref, o_ref):
    s = _silu(c_ref[...])
    o_ref[...] = _dot(_bf(s), _bf(w_ref[...])) + b_ref[...]


def _adaln(cvec, w_mod, b_mod):
    return pl.pallas_call(
        _adaln_kernel,
        grid=(DEPTH, 6 * D // ADA_TN),
        in_specs=[
            pl.BlockSpec((8, D), lambda l, n: (0, 0)),
            pl.BlockSpec((None, D, ADA_TN), lambda l, n: (l, 0, n)),
            pl.BlockSpec((None, 1, ADA_TN), lambda l, n: (l, 0, n)),
        ],
        out_specs=pl.BlockSpec((None, 8, ADA_TN), lambda l, n: (l, 0, n)),
        out_shape=jax.ShapeDtypeStruct((DEPTH, 8, 6 * D), F32),
        compiler_params=_cparams(("arbitrary", "arbitrary")),
        name="adaln",
    )(cvec, w_mod, b_mod.reshape(DEPTH, 1, 6 * D))


def _log_sigmoid(x):
    return jnp.minimum(x, 0.0) - jnp.log1p(jnp.exp(-jnp.abs(x)))


def _split3(x):
    p1 = _bf(x)
    r1 = x - p1.astype(F32)
    p2 = _bf(r1)
    p3 = _bf(r1 - p2.astype(F32))
    return p1, p2, p3


def _gla_kernel(nseq, t_seq, has_state, emit_state, *refs):
    it = iter(refs)
    x_ref, mod_ref, win_ref, w1_ref, w2_ref, b2_ref, gn_ref = (next(it) for _ in range(7))
    s0f_ref = s0b_ref = None
    if has_state:
        s0f_ref, s0b_ref = next(it), next(it)
    og_ref = next(it)
    sf_ref = sb_ref = None
    if emit_state:
        sf_ref, sb_ref = next(it), next(it)
    hb_scr, q_scr, k_scr, v_scr, r_scr, lgf_scr, lgb_scr, of_scr, ob_scr, st_scr = (next(it) for _ in range(10))

    nchunk = t_seq // CHUNK
    sh1 = mod_ref[0:1, :]
    sc1 = mod_ref[1:2, :]
    hb_scr[...] = _bf(x_ref[...] * (1.0 + sc1) + sh1)
    z1 = _bf(_dot(hb_scr[...], w1_ref[...]))

    row = lax.broadcasted_iota(I32, (CHUNK, CHUNK), 0)
    col = lax.broadcasted_iota(I32, (CHUNK, CHUNK), 1)
    lower = row >= col
    upper = row <= col
    l_pre = _bf(lower.astype(F32))
    l_suf = _bf(upper.astype(F32))

    def chain_step(rows, lg_scr, tri, mask, g_tot_row, st_idx, o_scr):
        lg = lg_scr[rows, :]
        p1, p2, p3 = _split3(lg)
        g3 = _dot(tri, jnp.concatenate([p1, p2, p3], axis=1))
        g = g3[:, 0:LANES] + g3[:, LANES:2 * LANES] + g3[:, 2 * LANES:3 * LANES]
        g_tot = g[g_tot_row:g_tot_row + 1, :]
        q = q_scr[rows, :]
        k = k_scr[rows, :]
        v = v_scr[rows, :]
        qg = _bf(q * jnp.exp(g))
        kg = _bf(k * jnp.exp(-g))
        kd = _bf(k * jnp.exp(g_tot - g))
        a = jnp.where(mask, _dot_nt(qg, kg), 0.0)
        st = st_scr[st_idx]
        o = _dot(_bf(a), v) + _dot_nt(qg, _bf(st))
        o_scr[rows, :] = o
        st_scr[st_idx] = jnp.exp(g_tot) * st + _dot_tn(v, kd)

    for h in range(GLA_HEADS):
        hb = hb_scr[...]
        q_scr[...] = _dot(hb, win_ref[:, h * GLA_DK:(h + 1) * GLA_DK]) * (GLA_DK ** -0.5)
        k_scr[...] = _dot(hb, win_ref[:, GLA_KD + h * GLA_DK:GLA_KD + (h + 1) * GLA_DK])
        v_scr[...] = _bf(_dot(hb, win_ref[:, 2 * GLA_KD + h * GLA_DV:2 * GLA_KD + (h + 1) * GLA_DV]))
        r_scr[...] = _dot(hb, win_ref[:, 2 * GLA_KD + GLA_VD + h * GLA_DV:2 * GLA_KD + GLA_VD + (h + 1) * GLA_DV])
        zf = _dot(z1, w2_ref[:, h * GLA_DK:(h + 1) * GLA_DK]) + b2_ref[:, h * GLA_DK:(h + 1) * GLA_DK]
        zb = (_dot(z1, w2_ref[:, GLA_KD + h * GLA_DK:GLA_KD + (h + 1) * GLA_DK])
              + b2_ref[:, GLA_KD + h * GLA_DK:GLA_KD + (h + 1) * GLA_DK])
        lgf_scr[...] = _log_sigmoid(zf) / GLA_TAU
        lgb_scr[...] = _log_sigmoid(zb) / GLA_TAU
        for s in range(nseq):
            if has_state:
                st_scr[2 * s] = s0f_ref[h].T
                st_scr[2 * s + 1] = s0b_ref[h].T
            else:
                st_scr[2 * s] = jnp.zeros((GLA_DV, GLA_DK), F32)
                st_scr[2 * s + 1] = jnp.zeros((GLA_DV, GLA_DK), F32)

        def step(c, carry):
            for s in range(nseq):
                rf = pl.ds(pl.multiple_of(s * t_seq + c * CHUNK, CHUNK), CHUNK)
                rb = pl.ds(pl.multiple_of(s * t_seq + (nchunk - 1 - c) * CHUNK, CHUNK), CHUNK)
                chain_step(rf, lgf_scr, l_pre, lower, CHUNK - 1, 2 * s, of_scr)
                chain_step(rb, lgb_scr, l_suf, upper, 0, 2 * s + 1, ob_scr)
            return carry

        lax.fori_loop(0, nchunk, step, 0)

        if emit_state:
            for s in range(nseq):
                sf_ref[s, h] = st_scr[2 * s].T
                sb_ref[s, h] = st_scr[2 * s + 1].T
        o = of_scr[...] + ob_scr[...]
        on = o * lax.rsqrt(jnp.mean(o * o, axis=-1, keepdims=True) + EPS) * gn_ref[:, h * GLA_DV:(h + 1) * GLA_DV]
        og_ref[:, h * GLA_DV:(h + 1) * GLA_DV] = _bf(on * _silu(r_scr[...]))


def _gla_call(x_all, mod0, win, w1, w2, b2, gn, state_f, state_b, *, prompt):
    nseq = GLA_BLK // SEQ if prompt else GLA_BLK // DEC_SEQ
    t_seq = SEQ if prompt else DEC_SEQ
    nblk = NP // GLA_BLK if prompt else NS // GLA_BLK
    blk0 = 0 if prompt else NP // GLA_BLK
    has_state = not prompt
    emit_state = prompt
    if prompt:
        mod_map = lambda i: (0, 0, 0)
    else:
        mod_map = lambda i: (1 + i, 0, 0)
    in_specs = [
        pl.BlockSpec((GLA_BLK, D), lambda i: (blk0 + i, 0)),
        pl.BlockSpec((None, 6, D), mod_map),
        pl.BlockSpec((D, 2 * GLA_KD + 2 * GLA_VD), lambda i: (0, 0)),
        pl.BlockSpec((D, 2 * GLA_RANK), lambda i: (0, 0)),
        pl.BlockSpec((2 * GLA_RANK, 2 * GLA_KD), lambda i: (0, 0)),
        pl.BlockSpec((1, 2 * GLA_KD), lambda i: (0, 0)),
        pl.BlockSpec((1, GLA_VD), lambda i: (0, 0)),
    ]
    args = [x_all, mod0, win, w1, w2, b2, gn]
    if has_state:
        in_specs += [pl.BlockSpec((None, GLA_HEADS, GLA_DK, GLA_DV), lambda i: (i, 0, 0, 0))] * 2
        args += [state_f, state_b]
    out_specs = [pl.BlockSpec((GLA_BLK, GLA_VD), lambda i: (i, 0))]
    out_shape = [jax.ShapeDtypeStruct((nblk * GLA_BLK, GLA_VD), BF16)]
    if emit_state:
        st_spec = pl.BlockSpec((nseq, None, GLA_HEADS, GLA_DK, GLA_DV), lambda i: (i, 0, 0, 0, 0))
        out_specs += [st_spec, st_spec]
        out_shape += [jax.ShapeDtypeStruct((BATCH, 1, GLA_HEADS, GLA_DK, GLA_DV), F32)] * 2
    scratch = [
        pltpu.VMEM((GLA_BLK, D), BF16),
        pltpu.VMEM((GLA_BLK, GLA_DK), F32),
        pltpu.VMEM((GLA_BLK, GLA_DK), F32),
        pltpu.VMEM((GLA_BLK, GLA_DV), BF16),
        pltpu.VMEM((GLA_BLK, GLA_DV), F32),
        pltpu.VMEM((GLA_BLK, GLA_DK), F32),
        pltpu.VMEM((GLA_BLK, GLA_DK), F32),
        pltpu.VMEM((GLA_BLK, GLA_DV), F32),
        pltpu.VMEM((GLA_BLK, GLA_DV), F32),
        pltpu.VMEM((2 * nseq, GLA_DV, GLA_DK), F32),
    ]
    return pl.pallas_call(
        functools.partial(_gla_kernel, nseq, t_seq, has_state, emit_state),
        grid=(nblk,),
        in_specs=in_specs,
        out_specs=out_specs,
        out_shape=out_shape,
        scratch_shapes=scratch,
        compiler_params=_cparams(("arbitrary",)),
        name="gla_prompt" if prompt else "gla_sample",
    )(*args)


def _post_mixer_kernel(op_ref, os_ref, wout_ref, x_ref, mod_ref, lng_ref, lnb_ref, wr_ref,
                       x1_ref, h2_ref, aff_ref):
    ga1 = mod_ref[2:3, :]
    sh2 = mod_ref[3:4, :]
    sc2 = mod_ref[4:5, :]
    o = jnp.where(pl.program_id(0) < NP // TM, op_ref[...], os_ref[...])
    m = _dot(o, wout_ref[...])
    x1 = _layer_norm(ALPHA * x_ref[...] + ga1 * m, lng_ref[...], lnb_ref[...])
    x1_ref[...] = x1
    h2 = x1 * (1.0 + sc2) + sh2
    h2_ref[...] = h2
    wr = wr_ref[...]
    a1 = _bf(wr)
    a2 = _bf(wr - a1.astype(F32))
    b1 = _bf(h2)
    b2 = _bf(h2 - b1.astype(F32))
    lt = _dot_nt(a1, b1) + (_dot_nt(a1, b2) + _dot_nt(a2, b1))
    ex = jnp.exp(lt - jnp.max(lt, axis=0, keepdims=True))
    aff_ref[...] = ex / jnp.sum(ex, axis=0, keepdims=True)


def _post_mixer(o_p, o_s, wout, x_all, mod_l, lng, lnb, wr_t):
    tok = pl.BlockSpec((TM, D), lambda i: (i, 0))
    return pl.pallas_call(
        _post_mixer_kernel,
        grid=(NT // TM,),
        in_specs=[
            pl.BlockSpec((TM, D), lambda i: (jnp.minimum(i, NP // TM - 1), 0)),
            pl.BlockSpec((TM, D), lambda i: (jnp.maximum(i - NP // TM, 0), 0)),
            pl.BlockSpec((D, D), lambda i: (0, 0)),
            tok,
            pl.BlockSpec((None, 6, D), lambda i: (_mod_row(i), 0, 0)),
            pl.BlockSpec((1, D), lambda i: (0, 0)),
            pl.BlockSpec((1, D), lambda i: (0, 0)),
            pl.BlockSpec((E, D), lambda i: (0, 0)),
        ],
        out_specs=[tok, tok, pl.BlockSpec((E, TM), lambda i: (0, i))],
        out_shape=[jax.ShapeDtypeStruct((NT, D), F32), jax.ShapeDtypeStruct((NT, D), F32),
                   jax.ShapeDtypeStruct((E, NT), F32)],
        compiler_params=_cparams(("arbitrary",)),
        name="post_mixer",
    )(o_p, o_s, wout, x_all, mod_l, lng, lnb, wr_t)


def _plan_kernel(n_tok, cap, tok_off, row_off, aff_ref, idx_ref, dst_ref, g_ref, offs_ref, cnt_ref,
                 sel_scr, csl_scr, cs_scr, rhi_scr, rlo_scr, g1_scr, g2_scr, g3_scr):
    nb = n_tok // LANES
    rows = E * nb
    aff = aff_ref[...]
    bits = pltpu.bitcast(aff, I32)

    thr = jnp.zeros((E, 1, 1), I32)
    for b in range(30, -1, -1):
        cand = thr | (1 << b)
        cnt = jnp.sum(jnp.sum((bits >= cand).astype(F32), axis=2, keepdims=True), axis=1, keepdims=True)
        thr = jnp.where(cnt >= cap, cand, thr)

    triu = _bf((lax.broadcasted_iota(I32, (LANES, LANES), 0)
                <= lax.broadcasted_iota(I32, (LANES, LANES), 1)).astype(F32))
    rr = lax.broadcasted_iota(I32, (rows, rows), 0)
    cc = lax.broadcasted_iota(I32, (rows, rows), 1)
    sh = int(math.log2(nb))
    same = lax.shift_right_logical(rr, sh) == lax.shift_right_logical(cc, sh)
    blk = _bf((same & (cc < rr)).astype(F32))

    def cumsum_tokens(m2):
        local = _dot(_bf(m2), triu)
        tot = jnp.broadcast_to(local[:, LANES - 1:LANES], (rows, LANES))
        base = _dot(blk, _bf(tot))
        return local, local + base

    gt = bits > thr
    eq = bits == thr
    n_gt = jnp.sum(jnp.sum(gt.astype(F32), axis=2, keepdims=True), axis=1, keepdims=True)
    need = jnp.broadcast_to(cap - n_gt, (E, nb, LANES)).reshape(rows, LANES)
    eq2 = eq.astype(F32).reshape(rows, LANES)
    gt2 = gt.astype(F32).reshape(rows, LANES)
    _, eq_incl = cumsum_tokens(eq2)
    sel2 = jnp.maximum(gt2, eq2 * ((eq_incl - eq2) < need).astype(F32))
    csl2, cs2 = cumsum_tokens(sel2)
    sel3 = sel2.reshape(E, nb, LANES)
    pos3 = (cs2 - sel2).reshape(E, nb, LANES)

    offs = jnp.sum(pos3, axis=0)
    cnt_ref[...] = jnp.sum(sel3, axis=0)
    offs_ref[...] = offs
    within = jnp.zeros((nb, LANES), F32)
    for e in range(E):
        r = offs + within + float(row_off)
        rhi = jnp.floor(r * (1.0 / LANES))
        rhi_scr[e, 0:nb, :] = rhi
        rlo_scr[e, 0:nb, :] = r - rhi * LANES
        within = within + sel3[e]
    sel_scr[:, 0:nb, :] = sel3
    csl_scr[:, 0:nb, :] = csl2.reshape(E, nb, LANES)
    cs_scr[:, 0:nb, :] = cs2.reshape(E, nb, LANES)
    p1, p2, p3 = _split3(aff)
    g1_scr[:, 0:nb, :] = p1.astype(F32)
    g2_scr[:, 0:nb, :] = p2.astype(F32)
    g3_scr[:, 0:nb, :] = p3.astype(F32)
    if nb < NBLK:
        zpad = jnp.zeros((E, NBLK - nb, LANES), F32)
        for scr in (sel_scr, csl_scr, rhi_scr, rlo_scr, g1_scr, g2_scr, g3_scr):
            scr[:, nb:NBLK, :] = zpad
        cs_scr[:, nb:NBLK, :] = jnp.full((E, NBLK - nb, LANES), 4.0 * cap, F32)

    jrow = lax.broadcasted_iota(I32, (1, cap), 1).astype(F32)
    sub = lax.broadcasted_iota(I32, (NBLK, cap), 0).astype(F32)

    def per_expert(e, carry):
        cs_e = cs_scr[e]
        csl_e = csl_scr[e]
        cs_end = cs_e[:, LANES - 1:LANES]
        base_e = cs_end - csl_e[:, LANES - 1:LANES]
        bidx = jnp.sum((cs_end <= jrow).astype(F32), axis=0, keepdims=True)
        ohb = sub == bidx
        jl = jrow - jnp.sum(jnp.where(ohb, base_e, 0.0), axis=0, keepdims=True)
        ohb_bf = _bf(ohb.astype(F32))
        crow = _dot(_bf(csl_e.T), ohb_bf)
        lidx = jnp.sum((crow <= jl).astype(F32), axis=0, keepdims=True)
        ohl = sub == lidx

        def pick(scr):
            prow = _dot(_bf(scr[e].T), ohb_bf)
            return jnp.sum(jnp.where(ohl, prow, 0.0), axis=0, keepdims=True)

        idx = bidx * LANES + lidx + float(tok_off)
        dst = pick(rhi_scr) * LANES + pick(rlo_scr)
        gate = pick(g1_scr) + pick(g2_scr) + pick(g3_scr)
        idx_ref[pl.ds(e, 1), :] = idx.astype(I32)
        dst_ref[pl.ds(e, 1), :] = dst.astype(I32)
        g_ref[pl.ds(e, 1), :] = gate
        return carry

    lax.fori_loop(0, E, per_expert, 0)


def _plan(aff3, n_tok, cap, tok_off, row_off, name):
    nb = n_tok // LANES
    scr = [pltpu.VMEM((E, NBLK, LANES), F32) for _ in range(8)]
    return pl.pallas_call(
        functools.partial(_plan_kernel, n_tok, cap, tok_off, row_off),
        out_shape=[jax.ShapeDtypeStruct((E, cap), I32), jax.ShapeDtypeStruct((E, cap), I32),
                   jax.ShapeDtypeStruct((E, cap), F32), jax.ShapeDtypeStruct((nb, LANES), F32),
                   jax.ShapeDtypeStruct((nb, LANES), F32)],
        scratch_shapes=scr,
        compiler_params=pltpu.CompilerParams(vmem_limit_bytes=VMEM_LIMIT),
        name=name,
    )(aff3)


def _expert_kernel(idx_ref, dst_ref, h_hbm, g_ref, wg_ref, wu_ref, wd_ref, z_hbm,
                   xe_scr, xb_scr, acc_scr, sem):
    e = pl.program_id(0)
    f = pl.program_id(1)
    nf = pl.num_programs(1)
    base = e * CT

    def row_copy_in(j):
        return pltpu.make_async_copy(h_hbm.at[pl.ds(idx_ref[base + j], 1), :], xe_scr.at[pl.ds(j, 1), :], sem.at[0])

    def row_copy_out(j):
        return pltpu.make_async_copy(xe_scr.at[pl.ds(j, 1), :], z_hbm.at[pl.ds(dst_ref[base + j], 1), :], sem.at[1])

    @pl.when(f == 0)
    def _gather():
        def start(j, c):
            row_copy_in(j).start()
            return c

        lax.fori_loop(0, CT, start, 0)
        pltpu.make_async_copy(h_hbm.at[pl.ds(0, CT), :], xe_scr, sem.at[0]).wait()
        xb_scr[...] = _bf(xe_scr[...])
        acc_scr[...] = jnp.zeros((CT, D), F32)

    xb = xb_scr[...]
    hg = _dot(xb, _bf(wg_ref[...]))
    hu = _dot(xb, _bf(wu_ref[...]))
    hid = _bf(_silu(hg) * hu)
    acc_scr[...] += _dot(hid, _bf(wd_ref[...]))

    @pl.when(f == nf - 1)
    def _scatter():
        xe_scr[...] = acc_scr[...] * g_ref[...]

        def start(j, c):
            row_copy_out(j).start()
            return c

        lax.fori_loop(0, CT, start, 0)
        pltpu.make_async_copy(xe_scr, z_hbm.at[pl.ds(0, CT), :], sem.at[1]).wait()


def _experts(idx_flat, dst_flat, h_all, g_col, w_gate, w_up, w_down, layer):
    grid_spec = pltpu.PrefetchScalarGridSpec(
        num_scalar_prefetch=2,
        grid=(E, FF // TF),
        in_specs=[
            pl.BlockSpec(memory_space=pl.ANY),
            pl.BlockSpec((None, CT, 1), lambda e, f, *_: (e, 0, 0)),
            pl.BlockSpec((None, None, D, TF), lambda e, f, *_: (layer, e, 0, f)),
            pl.BlockSpec((None, None, D, TF), lambda e, f, *_: (layer, e, 0, f)),
            pl.BlockSpec((None, None, TF, D), lambda e, f, *_: (layer, e, f, 0)),
        ],
        out_specs=pl.BlockSpec(memory_space=pl.ANY),
        scratch_shapes=[
            pltpu.VMEM((CT, D), F32),
            pltpu.VMEM((CT, D), BF16),
            pltpu.VMEM((CT, D), F32),
            pltpu.SemaphoreType.DMA((2,)),
        ],
    )
    return pl.pallas_call(
        _expert_kernel,
        grid_spec=grid_spec,
        out_shape=jax.ShapeDtypeStruct((ZROWS, D), F32),
        compiler_params=_cparams(("arbitrary", "arbitrary")),
        name="experts",
    )(idx_flat, dst_flat, h_all, g_col, w_gate, w_up, w_down)


def _combine_kernel(rs_ref, rn_ref, z_hbm, lo_ref, hi_ref, x1_ref, mod_ref, lng_ref, lnb_ref,
                    x2_ref, zbuf, acc_scr, sem):
    i = pl.program_id(0)
    r0 = rs_ref[i]
    rn = rn_ref[i]
    s8 = (r0 // 8) * 8
    nchunk = (r0 + rn - s8 + RK - 1) // RK
    lo = lo_ref[...]
    hi = hi_ref[...]
    acc_scr[...] = jnp.zeros((TC, D), F32)

    def chunk(c, carry):
        want = s8 + c * RK
        start = pl.multiple_of(jnp.minimum(want, ZROWS - RK), 8)
        cp = pltpu.make_async_copy(z_hbm.at[pl.ds(start, RK), :], zbuf, sem)
        cp.start()
        cp.wait()
        rowid = (start + lax.broadcasted_iota(I32, (1, RK), 1)).astype(F32)
        oh = (rowid >= lo) & (rowid < hi) & (rowid >= want.astype(F32))
        oh = _bf(oh.astype(F32))
        z = zbuf[...]
        z1 = _bf(z)
        z2 = _bf(z - z1.astype(F32))
        acc_scr[...] += _dot(oh, z1) + _dot(oh, z2)
        return carry

    lax.fori_loop(0, nchunk, chunk, 0)
    ga2 = mod_ref[5:6, :]
    x2_ref[...] = _layer_norm(ALPHA * x1_ref[...] + ga2 * acc_scr[...], lng_ref[...], lnb_ref[...])


def _combine(rs, rn, z, lo_col, hi_col, x1_all, mod_l, lng, lnb):
    ratio = TM // TC
    grid_spec = pltpu.PrefetchScalarGridSpec(
        num_scalar_prefetch=2,
        grid=(NT // TC,),
        in_specs=[
            pl.BlockSpec(memory_space=pl.ANY),
            pl.BlockSpec((TC, 1), lambda i, *_: (i, 0)),
            pl.BlockSpec((TC, 1), lambda i, *_: (i, 0)),
            pl.BlockSpec((TC, D), lambda i, *_: (i, 0)),
            pl.BlockSpec((None, 6, D), lambda i, *_: (_mod_row(i // ratio), 0, 0)),
            pl.BlockSpec((1, D), lambda i, *_: (0, 0)),
            pl.BlockSpec((1, D), lambda i, *_: (0, 0)),
        ],
        out_specs=pl.BlockSpec((TC, D), lambda i, *_: (i, 0)),
        scratch_shapes=[pltpu.VMEM((RK, D), F32), pltpu.VMEM((TC, D), F32), pltpu.SemaphoreType.DMA],
    )
    return pl.pallas_call(
        _combine_kernel,
        grid_spec=grid_spec,
        out_shape=jax.ShapeDtypeStruct((NT, D), F32),
        compiler_params=_cparams(("arbitrary",)),
        name="combine",
    )(rs, rn, z, lo_col, hi_col, x1_all, mod_l, lng, lnb)


def _moe(h2_all, aff_t, x1_all, mod_l, lng, lnb, w_gate, w_up, w_down, layer):
    aff_p = aff_t[:, :NP].reshape(E, NP // LANES, LANES)
    aff_s = aff_t[:, NP:].reshape(E, NS // LANES, LANES)
    idx_p, dst_p, g_p, offs_p, cnt_p = _plan(aff_p, NP, CP, 0, 0, "plan_prompt")
    idx_s, dst_s, g_s, offs_s, cnt_s = _plan(aff_s, NS, CS, NP, 2 * NP, "plan_sample")
    idx = jnp.concatenate([idx_p, idx_s], axis=1).reshape(E * CT)
    dst = jnp.concatenate([dst_p, dst_s], axis=1).reshape(E * CT)
    g_col = jnp.concatenate([g_p, g_s], axis=1).reshape(E, CT, 1)
    z = _experts(idx, dst, h2_all, g_col, w_gate, w_up, w_down, layer)
    lo = jnp.concatenate([offs_p.reshape(NP), offs_s.reshape(NS) + 2.0 * NP])
    hi = lo + jnp.concatenate([cnt_p.reshape(NP), cnt_s.reshape(NS)])
    lo_t = lo.reshape(NT // TC, TC)[:, 0]
    hi_t = hi.reshape(NT // TC, TC)[:, TC - 1]
    rs = lo_t.astype(I32)
    rn = (hi_t - lo_t).astype(I32)
    return _combine(rs, rn, z, lo.reshape(NT, 1), hi.reshape(NT, 1), x1_all, mod_l, lng, lnb)


def _rms_heads(x, g, nheads):
    outs = []
    for h in range(nheads):
        xh = x[:, h * HEAD_DIM:(h + 1) * HEAD_DIM]
        outs.append(xh * lax.rsqrt(jnp.mean(xh * xh, axis=-1, keepdims=True) + EPS) * g)
    return outs


def _rope(x, cos, sin):
    lane = lax.broadcasted_iota(I32, x.shape, 1)
    first_half = (lane % ROPE_AXIS_DIM) < ROPE_AXIS_DIM // 2
    partner = jnp.where(first_half, pltpu.roll(x, LANES - ROPE_AXIS_DIM // 2, 1), pltpu.roll(x, ROPE_AXIS_DIM // 2, 1))
    return x * cos + partner * sin


def _qkv_kernel(x_ref, mod_ref, w_ref, gq_ref, gk_ref, cos_ref, sin_ref, q_ref, k_ref, v_ref, kn_ref):
    i = pl.program_id(0)
    sh1 = mod_ref[0:1, :]
    sc1 = mod_ref[1:2, :]
    hb = _bf(x_ref[...] * (1.0 + sc1) + sh1)
    qkv = _dot(hb, w_ref[...])
    qs = _rms_heads(qkv[:, :ATTN_QD], gq_ref[...], ATTN_HEADS)
    ks = _rms_heads(qkv[:, ATTN_QD:ATTN_QD + ATTN_KVD], gk_ref[...], ATTN_KV)
    v_ref[...] = qkv[:, ATTN_QD + ATTN_KVD:]
    for h in range(ATTN_KV):
        kn_ref[:, h * HEAD_DIM:(h + 1) * HEAD_DIM] = ks[h]

    @pl.when(i < NP // TM)
    def _plain():
        for h in range(ATTN_HEADS):
            q_ref[:, h * HEAD_DIM:(h + 1) * HEAD_DIM] = qs[h]
        for h in range(ATTN_KV):
            k_ref[:, h * HEAD_DIM:(h + 1) * HEAD_DIM] = ks[h]

    @pl.when(i >= NP // TM)
    def _rotary():
        cos = cos_ref[...]
        sin = sin_ref[...]
        for h in range(ATTN_HEADS):
            q_ref[:, h * HEAD_DIM:(h + 1) * HEAD_DIM] = _rope(qs[h], cos, sin)
        for h in range(ATTN_KV):
            k_ref[:, h * HEAD_DIM:(h + 1) * HEAD_DIM] = _rope(ks[h], cos, sin)


def _qkv(x_all, mod_l, w, gq, gk, cos_t, sin_t):
    tok = lambda n: pl.BlockSpec((TM, n), lambda i: (i, 0))
    pos_map = lambda i: (jnp.maximum(i - NP // TM, 0) % (DEC_SEQ // TM), 0)
    return pl.pallas_call(
        _qkv_kernel,
        grid=(NT // TM,),
        in_specs=[
            tok(D),
            pl.BlockSpec((None, 6, D), lambda i: (_mod_row(i), 0, 0)),
            pl.BlockSpec((D, ATTN_QD + 2 * ATTN_KVD), lambda i: (0, 0)),
            pl.BlockSpec((1, HEAD_DIM), lambda i: (0, 0)),
            pl.BlockSpec((1, HEAD_DIM), lambda i: (0, 0)),
            pl.BlockSpec((TM, HEAD_DIM), pos_map),
            pl.BlockSpec((TM, HEAD_DIM), pos_map),
        ],
        out_specs=[tok(ATTN_QD), tok(ATTN_KVD), tok(ATTN_KVD), tok(ATTN_KVD)],
        out_shape=[jax.ShapeDtypeStruct((NT, ATTN_QD), F32), jax.ShapeDtypeStruct((NT, ATTN_KVD), F32),
                   jax.ShapeDtypeStruct((NT, ATTN_KVD), F32), jax.ShapeDtypeStruct((NT, ATTN_KVD), F32)],
        compiler_params=_cparams(("arbitrary",)),
        name="qkv",
    )(x_all, mod_l, w, gq, gk, cos_t, sin_t)


def _softmax_pv(s_list, v_list):
    mx = s_list[0].max(axis=-1, keepdims=True)
    for s in s_list[1:]:
        mx = jnp.maximum(mx, s.max(axis=-1, keepdims=True))
    ps = [jnp.exp(s - mx) for s in s_list]
    den = ps[0].sum(axis=-1, keepdims=True)
    for p in ps[1:]:
        den = den + p.sum(axis=-1, keepdims=True)
    out = None
    for p, v in zip(ps, v_list):
        o = _dot(_bf(p / den), v)
        out = o if out is None else out + o
    return out


QB = 256


def _attn_prompt_kernel(q_ref, k_ref, v_ref, o_ref):
    kb = _bf(k_ref[...])
    vb = _bf(v_ref[...])
    for g in range(ATTN_G):
        qh = _bf(q_ref[:, g * HEAD_DIM:(g + 1) * HEAD_DIM])
        s = _dot_nt(qh, kb) * (HEAD_DIM ** -0.5)
        o_ref[:, g * HEAD_DIM:(g + 1) * HEAD_DIM] = _bf(_softmax_pv([s], [vb]))


def _attn_sample_kernel(q_ref, k_ref, v_ref, ck_ref, cv_ref, o_ref):
    kb = _bf(k_ref[...])
    vb = _bf(v_ref[...])
    ckb = _bf(ck_ref[...])
    cvb = _bf(cv_ref[...])
    for g in range(ATTN_G):
        for qb in range(DEC_SEQ // QB):
            qh = _bf(q_ref[qb * QB:(qb + 1) * QB, g * HEAD_DIM:(g + 1) * HEAD_DIM])
            s_ctx = _dot_nt(qh, ckb) * (HEAD_DIM ** -0.5)
            s_lat = _dot_nt(qh, kb) * (HEAD_DIM ** -0.5)
            o_ref[qb * QB:(qb + 1) * QB, g * HEAD_DIM:(g + 1) * HEAD_DIM] = _bf(
                _softmax_pv([s_ctx, s_lat], [cvb, vb]))


def _attention(q_all, k_all, v_all, ctx_k, ctx_v):
    gw = ATTN_G * HEAD_DIM
    o_prompt = pl.pallas_call(
        _attn_prompt_kernel,
        grid=(BATCH, ATTN_KV),
        in_specs=[
            pl.BlockSpec((SEQ, gw), lambda b, h: (b, h)),
            pl.BlockSpec((SEQ, HEAD_DIM), lambda b, h: (b, h)),
            pl.BlockSpec((SEQ, HEAD_DIM), lambda b, h: (b, h)),
        ],
        out_specs=pl.BlockSpec((SEQ, gw), lambda b, h: (b, h)),
        out_shape=jax.ShapeDtypeStruct((NP, ATTN_QD), BF16),
        compiler_params=_cparams(("arbitrary", "arbitrary")),
        name="attn_prompt",
    )(q_all, k_all, v_all)
    b0 = NP // DEC_SEQ
    o_sample = pl.pallas_call(
        _attn_sample_kernel,
        grid=(DEC_BATCH, ATTN_KV),
        in_specs=[
            pl.BlockSpec((DEC_SEQ, gw), lambda b, h: (b0 + b, h)),
            pl.BlockSpec((DEC_SEQ, HEAD_DIM), lambda b, h: (b0 + b, h)),
            pl.BlockSpec((DEC_SEQ, HEAD_DIM), lambda b, h: (b0 + b, h)),
            pl.BlockSpec((None, None, PAST_LEN, HEAD_DIM), lambda b, h: (b, h, 0, 0)),
            pl.BlockSpec((None, None, PAST_LEN, HEAD_DIM), lambda b, h: (b, h, 0, 0)),
        ],
        out_specs=pl.BlockSpec((DEC_SEQ, gw), lambda b, h: (b, h)),
        out_shape=jax.ShapeDtypeStruct((NS, ATTN_QD), BF16),
        compiler_params=_cparams(("arbitrary", "arbitrary")),
        name="attn_sample",
    )(q_all, k_all, v_all, ctx_k, ctx_v)
    return o_prompt, o_sample


def _rope_tables():
    pos = jnp.arange(DEC_SEQ)
    rowp = (pos // GRID_W).astype(F32)
    colp = (pos % GRID_W).astype(F32)
    inv = ROPE_THETA ** (-jnp.arange(0, ROPE_AXIS_DIM, 2, dtype=F32) / ROPE_AXIS_DIM)
    a_row = rowp[:, None] * inv
    a_col = colp[:, None] * inv
    cos_t = jnp.concatenate([jnp.cos(a_row), jnp.cos(a_row), jnp.cos(a_col), jnp.cos(a_col)], axis=1)
    sin_t = jnp.concatenate([-jnp.sin(a_row), jnp.sin(a_row), -jnp.sin(a_col), jnp.sin(a_col)], axis=1)
    return cos_t, sin_t


def kernel(x_prompt, x_sample, state_gla_fwd, state_gla_bwd, cache_attn_k, cache_attn_v, c, c_ctx, w_mod, b_mod, ln_g, ln_b, w_gla_in, w_gla_gf1, w_gla_gf2, b_gla_gf, w_gla_gb1, w_gla_gb2, b_gla_gb, g_gla_norm, w_gla_out, w_attn_in, g_attn_q, g_attn_k, w_attn_out, w_router, w_moe_gate, w_moe_up, w_moe_down):
    x_all = jnp.concatenate([x_prompt.reshape(NP, D), x_sample.reshape(NS, D)], axis=0)
    cvec = jnp.concatenate([c_ctx[None, :], c, jnp.zeros((8 - NMOD, D), F32)], axis=0)
    mod = _adaln(cvec, w_mod, b_mod)[:, :NMOD].reshape(DEPTH, NMOD, 6, D)

    w1 = _bf(jnp.concatenate([w_gla_gf1[0], w_gla_gb1[0]], axis=1))
    zr = jnp.zeros((GLA_RANK, GLA_KD), F32)
    w2 = _bf(jnp.concatenate([jnp.concatenate([w_gla_gf2[0], zr], axis=1),
                              jnp.concatenate([zr, w_gla_gb2[0]], axis=1)], axis=0))
    b2 = jnp.concatenate([b_gla_gf[0], b_gla_gb[0]])[None, :]
    gn = g_gla_norm[0][None, :]
    win = _bf(w_gla_in[0])
    og_p, new_f, new_b = _gla_call(x_all, mod[0], win, w1, w2, b2, gn, None, None, prompt=True)
    (og_s,) = _gla_call(x_all, mod[0], win, w1, w2, b2, gn, state_gla_fwd[:, 0], state_gla_bwd[:, 0],
                        prompt=False)
    x1, h2, aff = _post_mixer(og_p, og_s, _bf(w_gla_out[0]), x_all, mod[0], ln_g[0, 0][None, :],
                              ln_b[0, 0][None, :], w_router[0].T)
    x_all = _moe(h2, aff, x1, mod[0], ln_g[0, 1][None, :], ln_b[0, 1][None, :],
                 w_moe_gate, w_moe_up, w_moe_down, 0)

    cos_t, sin_t = _rope_tables()
    q_all, k_all, v_all, kn_all = _qkv(x_all, mod[1], _bf(w_attn_in[0]), g_attn_q[0][None, :],
                                       g_attn_k[0][None, :], cos_t, sin_t)
    oa_p, oa_s = _attention(q_all, k_all, v_all, cache_attn_k[:, 0], cache_attn_v[:, 0])
    x1, h2, aff = _post_mixer(oa_p, oa_s, _bf(w_attn_out[0]), x_all, mod[1], ln_g[1, 0][None, :],
                              ln_b[1, 0][None, :], w_router[1].T)
    x_all = _moe(h2, aff, x1, mod[1], ln_g[1, 1][None, :], ln_b[1, 1][None, :],
                 w_moe_gate, w_moe_up, w_moe_down, 1)

    def heads(a):
        return a[:NP].reshape(BATCH, SEQ, ATTN_KV, HEAD_DIM).transpose(0, 2, 1, 3)[:, None]

    return (x_all[:NP].reshape(BATCH, SEQ, D), x_all[NP:].reshape(DEC_BATCH, DEC_SEQ, D),
            new_f, new_b, heads(kn_all), heads(v_all))
```

```python
import functools
import math

import jax
import jax.numpy as jnp
from jax import lax
from jax.experimental import pallas as pl
from jax.experimental.pallas import tpu as pltpu

F32 = jnp.float32
BF16 = jnp.bfloat16
I32 = jnp.int32

D = 1024
BATCH, SEQ = 32, 256
DEC_BATCH, DEC_SEQ = 2, 1024
PAST_LEN = 256
DEPTH = 2
GRID_W = 64
NP = BATCH * SEQ
NS = DEC_BATCH * DEC_SEQ
NT = NP + NS
NMOD = 1 + DEC_BATCH
GLA_HEADS, GLA_DK, GLA_DV = 4, 128, 256
GLA_KD, GLA_VD = 512, 1024
GLA_RANK = 16
GLA_TAU = 16.0
CHUNK = 64
ATTN_HEADS, ATTN_KV, HEAD_DIM = 8, 2, 128
ATTN_G = ATTN_HEADS // ATTN_KV
ATTN_QD, ATTN_KVD = 1024, 256
ROPE_AXIS_DIM = 64
ROPE_THETA = 10000.0
E = 16
CP = 2 * NP // E
CS = 2 * NS // E
CT = CP + CS
FF = 2 * D
ALPHA = (2 * DEPTH) ** 0.25
EPS = 1e-6

LANES = 128
VMEM_LIMIT = 56 * 1024 * 1024
TM = 512
GLA_BLK = 1024
TF = 512
TC = 256
RK = 256
ZROWS = 2 * NT
NBLK = 128
SLAB = D // LANES


def _cparams(sem):
    return pltpu.CompilerParams(dimension_semantics=sem, vmem_limit_bytes=VMEM_LIMIT)


def _bf(x):
    return x.astype(BF16)


def _dot(a, b):
    return jnp.dot(a, b, preferred_element_type=F32)


def _dot_nt(a, b):
    return lax.dot_general(a, b, (((1,), (1,)), ((), ())), preferred_element_type=F32)


def _dot_tn(a, b):
    return lax.dot_general(a, b, (((0,), (0,)), ((), ())), preferred_element_type=F32)


def _layer_norm(y, g, b):
    mu = jnp.mean(y, axis=-1, keepdims=True)
    yc = y - mu
    var = jnp.mean(yc * yc, axis=-1, keepdims=True)
    return yc * lax.rsqrt(var + EPS) * g + b


def _silu(x):
    return x * jax.nn.sigmoid(x)


def _mod_row(i):
    return jnp.where(i < NP // TM, 0, 1 + (i - NP // TM) // (DEC_SEQ // TM))


ADA_TN = 1024


def _adaln_kernel(c_ref, w_ref, b_ref, o_ref):
    s = _silu(c_ref[...])
    o_ref[...] = _dot(_bf(s), _bf(w_ref[...])) + b_ref[...]


def _adaln(cvec, w_mod, b_mod):
    return pl.pallas_call(
        _adaln_kernel,
        grid=(DEPTH, 6 * D // ADA_TN),
        in_specs=[
            pl.BlockSpec((8, D), lambda l, n: (0, 0)),
            pl.BlockSpec((None, D, ADA_TN), lambda l, n: (l, 0, n)),
            pl.BlockSpec((None, 1, ADA_TN), lambda l, n: (l, 0, n)),
        ],
        out_specs=pl.BlockSpec((None, 8, ADA_TN), lambda l, n: (l, 0, n)),
        out_shape=jax.ShapeDtypeStruct((DEPTH, 8, 6 * D), F32),
        compiler_params=_cparams(("arbitrary", "arbitrary")),
        name="adaln",
    )(cvec, w_mod, b_mod.reshape(DEPTH, 1, 6 * D))


def _log_sigmoid(x):
    return jnp.minimum(x, 0.0) - jnp.log1p(jnp.exp(-jnp.abs(x)))


def _split3(x):
    p1 = _bf(x)
    r1 = x - p1.astype(F32)
    p2 = _bf(r1)
    p3 = _bf(r1 - p2.astype(F32))
    return p1, p2, p3


def _gla_kernel(nseq, t_seq, has_state, emit_state, *refs):
    it = iter(refs)
    x_ref, mod_ref, win_ref, w1_ref, w2_ref, b2_ref, gn_ref = (next(it) for _ in range(7))
    s0f_ref = s0b_ref = None
    if has_state:
        s0f_ref, s0b_ref = next(it), next(it)
    og_ref = next(it)
    sf_ref = sb_ref = None
    if emit_state:
        sf_ref, sb_ref = next(it), next(it)
    hb_scr, q_scr, k_scr, v_scr, r_scr, lgf_scr, lgb_scr, of_scr, ob_scr, st_scr = (next(it) for _ in range(10))

    nchunk = t_seq // CHUNK
    sh1 = mod_ref[0:1, :]
    sc1 = mod_ref[1:2, :]
    hb_scr[...] = _bf(x_ref[...] * (1.0 + sc1) + sh1)
    z1 = _bf(_dot(hb_scr[...], w1_ref[...]))

    row = lax.broadcasted_iota(I32, (CHUNK, CHUNK), 0)
    col = lax.broadcasted_iota(I32, (CHUNK, CHUNK), 1)
    lower = row >= col
    upper = row <= col
    l_pre = _bf(lower.astype(F32))
    l_suf = _bf(upper.astype(F32))

    def chain_step(rows, lg_scr, tri, mask, g_tot_row, st_idx, o_scr):
        lg = lg_scr[rows, :]
        p1, p2, p3 = _split3(lg)
        g3 = _dot(tri, jnp.concatenate([p1, p2, p3], axis=1))
        g = g3[:, 0:LANES] + g3[:, LANES:2 * LANES] + g3[:, 2 * LANES:3 * LANES]
        g_tot = g[g_tot_row:g_tot_row + 1, :]
        q = q_scr[rows, :]
        k = k_scr[rows, :]
        v = v_scr[rows, :]
        qg = _bf(q * jnp.exp(g))
        kg = _bf(k * jnp.exp(-g))
        kd = _bf(k * jnp.exp(g_tot - g))
        a = jnp.where(mask, _dot_nt(qg, kg), 0.0)
        st = st_scr[st_idx]
        o = _dot(_bf(a), v) + _dot_nt(qg, _bf(st))
        o_scr[rows, :] = o
        st_scr[st_idx] = jnp.exp(g_tot) * st + _dot_tn(v, kd)

    for h in range(GLA_HEADS):
        hb = hb_scr[...]
        q_scr[...] = _dot(hb, win_ref[:, h * GLA_DK:(h + 1) * GLA_DK]) * (GLA_DK ** -0.5)
        k_scr[...] = _dot(hb, win_ref[:, GLA_KD + h * GLA_DK:GLA_KD + (h + 1) * GLA_DK])
        v_scr[...] = _bf(_dot(hb, win_ref[:, 2 * GLA_KD + h * GLA_DV:2 * GLA_KD + (h + 1) * GLA_DV]))
        r_scr[...] = _dot(hb, win_ref[:, 2 * GLA_KD + GLA_VD + h * GLA_DV:2 * GLA_KD + GLA_VD + (h + 1) * GLA_DV])
        zf = _dot(z1, w2_ref[:, h * GLA_DK:(h + 1) * GLA_DK]) + b2_ref[:, h * GLA_DK:(h + 1) * GLA_DK]
        zb = (_dot(z1, w2_ref[:, GLA_KD + h * GLA_DK:GLA_KD + (h + 1) * GLA_DK])
              + b2_ref[:, GLA_KD + h * GLA_DK:GLA_KD + (h + 1) * GLA_DK])
        lgf_scr[...] = _log_sigmoid(zf) / GLA_TAU
        lgb_scr[...] = _log_sigmoid(zb) / GLA_TAU
        for s in range(nseq):
            if has_state:
                st_scr[2 * s] = s0f_ref[h].T
                st_scr[2 * s + 1] = s0b_ref[h].T
            else:
                st_scr[2 * s] = jnp.zeros((GLA_DV, GLA_DK), F32)
                st_scr[2 * s + 1] = jnp.zeros((GLA_DV, GLA_DK), F32)

        def step(c, carry):
            for s in range(nseq):
                rf = pl.ds(pl.multiple_of(s * t_seq + c * CHUNK, CHUNK), CHUNK)
                rb = pl.ds(pl.multiple_of(s * t_seq + (nchunk - 1 - c) * CHUNK, CHUNK), CHUNK)
                chain_step(rf, lgf_scr, l_pre, lower, CHUNK - 1, 2 * s, of_scr)
                chain_step(rb, lgb_scr, l_suf, upper, 0, 2 * s + 1, ob_scr)
            return carry

        lax.fori_loop(0, nchunk, step, 0)

        if emit_state:
            for s in range(nseq):
                sf_ref[s, h] = st_scr[2 * s].T
                sb_ref[s, h] = st_scr[2 * s + 1].T
        o = of_scr[...] + ob_scr[...]
        on = o * lax.rsqrt(jnp.mean(o * o, axis=-1, keepdims=True) + EPS) * gn_ref[:, h * GLA_DV:(h + 1) * GLA_DV]
        og_ref[:, h * GLA_DV:(h + 1) * GLA_DV] = _bf(on * _silu(r_scr[...]))


def _gla_call(x_stream, mod0, win, w1, w2, b2, gn, state_f, state_b, *, prompt):
    nseq = GLA_BLK // SEQ if prompt else GLA_BLK // DEC_SEQ
    t_seq = SEQ if prompt else DEC_SEQ
    nblk = NP // GLA_BLK if prompt else NS // GLA_BLK
    has_state = not prompt
    emit_state = prompt
    if prompt:
        mod_map = lambda i: (0, 0, 0)
    else:
        mod_map = lambda i: (1 + i, 0, 0)
    in_specs = [
        pl.BlockSpec((GLA_BLK, D), lambda i: (i, 0)),
        pl.BlockSpec((None, 6, D), mod_map),
        pl.BlockSpec((D, 2 * GLA_KD + 2 * GLA_VD), lambda i: (0, 0)),
        pl.BlockSpec((D, 2 * GLA_RANK), lambda i: (0, 0)),
        pl.BlockSpec((2 * GLA_RANK, 2 * GLA_KD), lambda i: (0, 0)),
        pl.BlockSpec((1, 2 * GLA_KD), lambda i: (0, 0)),
        pl.BlockSpec((1, GLA_VD), lambda i: (0, 0)),
    ]
    args = [x_stream, mod0, win, w1, w2, b2, gn]
    if has_state:
        in_specs += [pl.BlockSpec((None, GLA_HEADS, GLA_DK, GLA_DV), lambda i: (i, 0, 0, 0))] * 2
        args += [state_f, state_b]
    out_specs = [pl.BlockSpec((GLA_BLK, GLA_VD), lambda i: (i, 0))]
    out_shape = [jax.ShapeDtypeStruct((nblk * GLA_BLK, GLA_VD), BF16)]
    if emit_state:
        st_spec = pl.BlockSpec((nseq, None, GLA_HEADS, GLA_DK, GLA_DV), lambda i: (i, 0, 0, 0, 0))
        out_specs += [st_spec, st_spec]
        out_shape += [jax.ShapeDtypeStruct((BATCH, 1, GLA_HEADS, GLA_DK, GLA_DV), F32)] * 2
    scratch = [
        pltpu.VMEM((GLA_BLK, D), BF16),
        pltpu.VMEM((GLA_BLK, GLA_DK), F32),
        pltpu.VMEM((GLA_BLK, GLA_DK), F32),
        pltpu.VMEM((GLA_BLK, GLA_DV), BF16),
        pltpu.VMEM((GLA_BLK, GLA_DV), F32),
        pltpu.VMEM((GLA_BLK, GLA_DK), F32),
        pltpu.VMEM((GLA_BLK, GLA_DK), F32),
        pltpu.VMEM((GLA_BLK, GLA_DV), F32),
        pltpu.VMEM((GLA_BLK, GLA_DV), F32),
        pltpu.VMEM((2 * nseq, GLA_DV, GLA_DK), F32),
    ]
    return pl.pallas_call(
        functools.partial(_gla_kernel, nseq, t_seq, has_state, emit_state),
        grid=(nblk,),
        in_specs=in_specs,
        out_specs=out_specs,
        out_shape=out_shape,
        scratch_shapes=scratch,
        compiler_params=_cparams(("arbitrary",)),
        name="gla_prompt" if prompt else "gla_sample",
    )(*args)


def _post_mixer_kernel(split_x, *refs):
    it = iter(refs)
    op_ref, os_ref, wout_ref = next(it), next(it), next(it)
    is_prompt = pl.program_id(0) < NP // TM
    if split_x:
        xp_ref, xs_ref = next(it), next(it)
        x = jnp.where(is_prompt, xp_ref[...], xs_ref[...])
    else:
        x = next(it)[...]
    mod_ref, lng_ref, lnb_ref, wr_ref, x1_ref, h2_ref, aff_ref = (next(it) for _ in range(7))
    ga1 = mod_ref[2:3, :]
    sh2 = mod_ref[3:4, :]
    sc2 = mod_ref[4:5, :]
    o = jnp.where(is_prompt, op_ref[...], os_ref[...])
    m = _dot(o, wout_ref[...])
    x1 = _layer_norm(ALPHA * x + ga1 * m, lng_ref[...], lnb_ref[...])
    x1_ref[...] = x1
    h2 = x1 * (1.0 + sc2) + sh2
    for s in range(SLAB):
        h2_ref[pl.ds(s, TM, stride=SLAB), :] = h2[:, s * LANES:(s + 1) * LANES]
    wr = wr_ref[...]
    a1 = _bf(wr)
    a2 = _bf(wr - a1.astype(F32))
    b1 = _bf(h2)
    b2 = _bf(h2 - b1.astype(F32))
    lt = _dot_nt(a1, b1) + (_dot_nt(a1, b2) + _dot_nt(a2, b1))
    ex = jnp.exp(lt - jnp.max(lt, axis=0, keepdims=True))
    aff_ref[...] = ex / jnp.sum(ex, axis=0, keepdims=True)


def _post_mixer(o_p, o_s, wout, xs, mod_l, lng, lnb, wr_t):
    tok = pl.BlockSpec((TM, D), lambda i: (i, 0))
    tok_p = pl.BlockSpec((TM, D), lambda i: (jnp.minimum(i, NP // TM - 1), 0))
    tok_s = pl.BlockSpec((TM, D), lambda i: (jnp.maximum(i - NP // TM, 0), 0))
    split_x = len(xs) == 2
    return pl.pallas_call(
        functools.partial(_post_mixer_kernel, split_x),
        grid=(NT // TM,),
        in_specs=[
            tok_p,
            tok_s,
            pl.BlockSpec((D, D), lambda i: (0, 0)),
            *([tok_p, tok_s] if split_x else [tok]),
            pl.BlockSpec((None, 6, D), lambda i: (_mod_row(i), 0, 0)),
            pl.BlockSpec((1, D), lambda i: (0, 0)),
            pl.BlockSpec((1, D), lambda i: (0, 0)),
            pl.BlockSpec((E, D), lambda i: (0, 0)),
        ],
        out_specs=[tok, pl.BlockSpec((TM * SLAB, LANES), lambda i: (i, 0)), pl.BlockSpec((E, TM), lambda i: (0, i))],
        out_shape=[jax.ShapeDtypeStruct((NT, D), F32), jax.ShapeDtypeStruct((NT * SLAB, LANES), F32),
                   jax.ShapeDtypeStruct((E, NT), F32)],
        compiler_params=_cparams(("arbitrary",)),
        name="post_mixer",
    )(o_p, o_s, wout, *xs, mod_l, lng, lnb, wr_t)


def _plan_kernel(n_tok, cap, tok_off, row_off, aff_ref, idx_ref, dst_ref, g_ref, offs_ref, cnt_ref,
                 sel_scr, csl_scr, cs_scr, rhi_scr, rlo_scr, g1_scr, g2_scr, g3_scr):
    nb = n_tok // LANES
    rows = E * nb
    aff = aff_ref[...]

    thr_bits = jnp.zeros((E, 1, 1), I32)
    for b in range(30, -1, -1):
        cand = thr_bits | (1 << b)
        cand_f = lax.bitcast_convert_type(cand, F32)
        cnt = jnp.sum(jnp.sum((aff >= cand_f).astype(F32), axis=2, keepdims=True), axis=1, keepdims=True)
        thr_bits = jnp.where(cnt >= cap, cand, thr_bits)
    thr = lax.bitcast_convert_type(thr_bits, F32)

    triu = _bf((lax.broadcasted_iota(I32, (LANES, LANES), 0)
                <= lax.broadcasted_iota(I32, (LANES, LANES), 1)).astype(F32))
    rr = lax.broadcasted_iota(I32, (rows, rows), 0)
    cc = lax.broadcasted_iota(I32, (rows, rows), 1)
    sh = int(math.log2(nb))
    same = lax.shift_right_logical(rr, sh) == lax.shift_right_logical(cc, sh)
    blk = _bf((same & (cc < rr)).astype(F32))

    def cumsum_tokens(m2):
        local = _dot(_bf(m2), triu)
        tot = jnp.broadcast_to(local[:, LANES - 1:LANES], (rows, LANES))
        base = _dot(blk, _bf(tot))
        return local, local + base

    gt = aff > thr
    eq = aff == thr
    n_gt = jnp.sum(jnp.sum(gt.astype(F32), axis=2, keepdims=True), axis=1, keepdims=True)
    need = jnp.broadcast_to(cap - n_gt, (E, nb, LANES)).reshape(rows, LANES)
    eq2 = eq.astype(F32).reshape(rows, LANES)
    gt2 = gt.astype(F32).reshape(rows, LANES)
    _, eq_incl = cumsum_tokens(eq2)
    sel2 = jnp.maximum(gt2, eq2 * ((eq_incl - eq2) < need).astype(F32))
    csl2, cs2 = cumsum_tokens(sel2)
    sel3 = sel2.reshape(E, nb, LANES)
    pos3 = (cs2 - sel2).reshape(E, nb, LANES)

    offs = jnp.sum(pos3, axis=0)
    cnt_ref[...] = jnp.sum(sel3, axis=0)
    offs_ref[...] = offs
    within = jnp.zeros((nb, LANES), F32)
    for e in range(E):
        r = offs + within + float(row_off)
        rhi = jnp.floor(r * (1.0 / LANES))
        rhi_scr[e, 0:nb, :] = rhi
        rlo_scr[e, 0:nb, :] = r - rhi * LANES
        within = within + sel3[e]
    sel_scr[:, 0:nb, :] = sel3
    csl_scr[:, 0:nb, :] = csl2.reshape(E, nb, LANES)
    cs_scr[:, 0:nb, :] = cs2.reshape(E, nb, LANES)
    p1, p2, p3 = _split3(aff)
    g1_scr[:, 0:nb, :] = p1.astype(F32)
    g2_scr[:, 0:nb, :] = p2.astype(F32)
    g3_scr[:, 0:nb, :] = p3.astype(F32)
    if nb < NBLK:
        zpad = jnp.zeros((E, NBLK - nb, LANES), F32)
        for scr in (sel_scr, csl_scr, rhi_scr, rlo_scr, g1_scr, g2_scr, g3_scr):
            scr[:, nb:NBLK, :] = zpad
        cs_scr[:, nb:NBLK, :] = jnp.full((E, NBLK - nb, LANES), 4.0 * cap, F32)

    jrow = lax.broadcasted_iota(I32, (1, cap), 1).astype(F32)
    sub = lax.broadcasted_iota(I32, (NBLK, cap), 0).astype(F32)

    def per_expert(e, carry):
        cs_e = cs_scr[e]
        csl_e = csl_scr[e]
        cs_end = cs_e[:, LANES - 1:LANES]
        base_e = cs_end - csl_e[:, LANES - 1:LANES]
        bidx = jnp.sum((cs_end <= jrow).astype(F32), axis=0, keepdims=True)
        ohb = sub == bidx
        jl = jrow - jnp.sum(jnp.where(ohb, base_e, 0.0), axis=0, keepdims=True)
        ohb_bf = _bf(ohb.astype(F32))
        crow = _dot(_bf(csl_e.T), ohb_bf)
        lidx = jnp.sum((crow <= jl).astype(F32), axis=0, keepdims=True)
        ohl = sub == lidx

        def pick(scr):
            prow = _dot(_bf(scr[e].T), ohb_bf)
            return jnp.sum(jnp.where(ohl, prow, 0.0), axis=0, keepdims=True)

        idx = bidx * LANES + lidx + float(tok_off)
        dst = pick(rhi_scr) * LANES + pick(rlo_scr)
        gate = pick(g1_scr) + pick(g2_scr) + pick(g3_scr)
        idx_ref[pl.ds(e, 1), :] = idx.astype(I32)
        dst_ref[pl.ds(e, 1), :] = dst.astype(I32)
        g_ref[pl.ds(e, 1), :] = gate
        return carry

    lax.fori_loop(0, E, per_expert, 0)


def _plan(aff3, n_tok, cap, tok_off, row_off, name):
    nb = n_tok // LANES
    scr = [pltpu.VMEM((E, NBLK, LANES), F32) for _ in range(8)]
    return pl.pallas_call(
        functools.partial(_plan_kernel, n_tok, cap, tok_off, row_off),
        out_shape=[jax.ShapeDtypeStruct((E, cap), I32), jax.ShapeDtypeStruct((E, cap), I32),
                   jax.ShapeDtypeStruct((E, cap), F32), jax.ShapeDtypeStruct((nb, LANES), F32),
                   jax.ShapeDtypeStruct((nb, LANES), F32)],
        scratch_shapes=scr,
        compiler_params=pltpu.CompilerParams(vmem_limit_bytes=VMEM_LIMIT),
        name=name,
    )(aff3)


DMA_UNROLL = 8


def _expert_kernel(idx_ref, dst_ref, h_hbm, g_ref, wg_ref, wu_ref, wd_ref, z_hbm,
                   xe_scr, xb_scr, acc_scr, ye_scr, gsem, ssem):
    e = pl.program_id(0)
    f = pl.program_id(1)
    ne = pl.num_programs(0)
    nf = pl.num_programs(1)

    def slab(r):
        return pl.ds(pl.multiple_of(r * SLAB, SLAB), SLAB)

    def start_gather(ex, slot):
        def start(j, c):
            pltpu.make_async_copy(h_hbm.at[slab(idx_ref[ex * CT + j]), :],
                                  xe_scr.at[slot, slab(j), :], gsem.at[slot]).start()
            return c

        lax.fori_loop(0, CT, start, 0, unroll=DMA_UNROLL)

    def wait_gather(slot):
        pltpu.make_async_copy(h_hbm.at[pl.ds(0, CT * SLAB), :], xe_scr.at[slot], gsem.at[slot]).wait()

    def wait_scatter():
        pltpu.make_async_copy(ye_scr, z_hbm.at[pl.ds(0, CT * SLAB), :], ssem).wait()

    @pl.when((f == 0) & (e == 0))
    def _first_gather():
        start_gather(0, 0)

    @pl.when(f == 0)
    def _rows_ready():
        slot = e % 2
        wait_gather(slot)
        xe = xe_scr.at[slot]
        for s in range(SLAB):
            xb_scr[:, s * LANES:(s + 1) * LANES] = _bf(xe[pl.ds(s, CT, stride=SLAB), :])
        acc_scr[...] = jnp.zeros((CT, D), F32)

        @pl.when(e + 1 < ne)
        def _prefetch():
            start_gather(e + 1, 1 - slot)

    xb = xb_scr[...]
    hg = _dot(xb, _bf(wg_ref[...]))
    hu = _dot(xb, _bf(wu_ref[...]))
    hid = _bf(_silu(hg) * hu)
    acc_scr[...] += _dot(hid, _bf(wd_ref[...]))

    @pl.when(f == nf - 1)
    def _scatter():
        @pl.when(e > 0)
        def _staging_free():
            wait_scatter()

        g = g_ref[...]
        for s in range(SLAB):
            ye_scr[pl.ds(s, CT, stride=SLAB), :] = acc_scr[:, s * LANES:(s + 1) * LANES] * g

        def start(j, c):
            pltpu.make_async_copy(ye_scr.at[slab(j), :], z_hbm.at[slab(dst_ref[e * CT + j]), :], ssem).start()
            return c

        lax.fori_loop(0, CT, start, 0, unroll=DMA_UNROLL)

        @pl.when(e == ne - 1)
        def _drain():
            wait_scatter()


def _experts(idx_flat, dst_flat, h_all, g_col, w_gate, w_up, w_down, layer):
    grid_spec = pltpu.PrefetchScalarGridSpec(
        num_scalar_prefetch=2,
        grid=(E, FF // TF),
        in_specs=[
            pl.BlockSpec(memory_space=pl.ANY),
            pl.BlockSpec((None, CT, 1), lambda e, f, *_: (e, 0, 0)),
            pl.BlockSpec((None, None, D, TF), lambda e, f, *_: (layer, e, 0, f)),
            pl.BlockSpec((None, None, D, TF), lambda e, f, *_: (layer, e, 0, f)),
            pl.BlockSpec((None, None, TF, D), lambda e, f, *_: (layer, e, f, 0)),
        ],
        out_specs=pl.BlockSpec(memory_space=pl.ANY),
        scratch_shapes=[
            pltpu.VMEM((2, CT * SLAB, LANES), F32),
            pltpu.VMEM((CT, D), BF16),
            pltpu.VMEM((CT, D), F32),
            pltpu.VMEM((CT * SLAB, LANES), F32),
            pltpu.SemaphoreType.DMA((2,)),
            pltpu.SemaphoreType.DMA,
        ],
    )
    return pl.pallas_call(
        _expert_kernel,
        grid_spec=grid_spec,
        out_shape=jax.ShapeDtypeStruct((ZROWS * SLAB, LANES), F32),
        compiler_params=_cparams(("arbitrary", "arbitrary")),
        name="experts",
    )(idx_flat, dst_flat, h_all, g_col, w_gate, w_up, w_down)


def _combine_kernel(split_out, rs_ref, rn_ref, z_hbm, lo_ref, hi_ref, x1_ref, mod_ref, lng_ref, lnb_ref, *refs):
    if split_out:
        xp_ref, xs_ref, zbuf, acc_scr, sem = refs
    else:
        x2_ref, zbuf, acc_scr, sem = refs
    i = pl.program_id(0)
    nt = pl.num_programs(0)

    def first_row(t):
        return lax.shift_left(lax.shift_right_logical(rs_ref[t], 3), 3)

    def chunk_copy(t, c, slot):
        start = jnp.minimum(first_row(t) + c * RK, ZROWS - RK)
        rows = pl.ds(pl.multiple_of(start * SLAB, 8 * SLAB), RK * SLAB)
        return pltpu.make_async_copy(z_hbm.at[rows, :], zbuf.at[slot], sem.at[slot])

    s8 = first_row(i)
    nchunk = jnp.maximum((rs_ref[i] + rn_ref[i] - s8 + RK - 1) // RK, 1)
    lo = lo_ref[...]
    hi = hi_ref[...]
    acc_scr[...] = jnp.zeros((TC, D), F32)

    @pl.when(i == 0)
    def _first():
        chunk_copy(0, 0, 0).start()

    def chunk(c, carry):
        slot = c % 2
        chunk_copy(i, c, slot).wait()

        @pl.when(c + 1 < nchunk)
        def _next():
            chunk_copy(i, c + 1, 1 - slot).start()

        want = s8 + c * RK
        start = jnp.minimum(want, ZROWS - RK)
        rowid = (start + lax.broadcasted_iota(I32, (1, RK), 1)).astype(F32)
        oh = (rowid >= lo) & (rowid < hi) & (rowid >= want.astype(F32))
        oh = _bf(oh.astype(F32))
        zs = zbuf.at[slot]
        z = jnp.concatenate([zs[pl.ds(s, RK, stride=SLAB), :] for s in range(SLAB)], axis=1)
        z1 = _bf(z)
        z2 = _bf(z - z1.astype(F32))
        acc_scr[...] += _dot(oh, z1) + _dot(oh, z2)
        return carry

    lax.fori_loop(0, nchunk, chunk, 0)

    @pl.when(i + 1 < nt)
    def _prefetch_next_tile():
        chunk_copy(jnp.minimum(i + 1, nt - 1), 0, 0).start()

    ga2 = mod_ref[5:6, :]
    x2 = _layer_norm(ALPHA * x1_ref[...] + ga2 * acc_scr[...], lng_ref[...], lnb_ref[...])
    if split_out:
        @pl.when(i < NP // TC)
        def _prompt():
            xp_ref[...] = x2

        @pl.when(i >= NP // TC)
        def _sample():
            xs_ref[...] = x2
    else:
        x2_ref[...] = x2


def _combine(rs, rn, z, lo_col, hi_col, x1_all, mod_l, lng, lnb, split_out):
    ratio = TM // TC
    if split_out:
        out_specs = [pl.BlockSpec((TC, D), lambda i, *_: (jnp.minimum(i, NP // TC - 1), 0)),
                     pl.BlockSpec((TC, D), lambda i, *_: (jnp.maximum(i - NP // TC, 0), 0))]
        out_shape = [jax.ShapeDtypeStruct((NP, D), F32), jax.ShapeDtypeStruct((NS, D), F32)]
    else:
        out_specs = pl.BlockSpec((TC, D), lambda i, *_: (i, 0))
        out_shape = jax.ShapeDtypeStruct((NT, D), F32)
    grid_spec = pltpu.PrefetchScalarGridSpec(
        num_scalar_prefetch=2,
        grid=(NT // TC,),
        in_specs=[
            pl.BlockSpec(memory_space=pl.ANY),
            pl.BlockSpec((TC, 1), lambda i, *_: (i, 0)),
            pl.BlockSpec((TC, 1), lambda i, *_: (i, 0)),
            pl.BlockSpec((TC, D), lambda i, *_: (i, 0)),
            pl.BlockSpec((None, 6, D), lambda i, *_: (_mod_row(i // ratio), 0, 0)),
            pl.BlockSpec((1, D), lambda i, *_: (0, 0)),
            pl.BlockSpec((1, D), lambda i, *_: (0, 0)),
        ],
        out_specs=out_specs,
        scratch_shapes=[pltpu.VMEM((2, RK * SLAB, LANES), F32), pltpu.VMEM((TC, D), F32),
                        pltpu.SemaphoreType.DMA((2,))],
    )
    return pl.pallas_call(
        functools.partial(_combine_kernel, split_out),
        grid_spec=grid_spec,
        out_shape=out_shape,
        compiler_params=_cparams(("arbitrary",)),
        name="combine",
    )(rs, rn, z, lo_col, hi_col, x1_all, mod_l, lng, lnb)


def _moe(h2_all, aff_t, x1_all, mod_l, lng, lnb, w_gate, w_up, w_down, layer, split_out):
    aff_p = aff_t[:, :NP].reshape(E, NP // LANES, LANES)
    aff_s = aff_t[:, NP:].reshape(E, NS // LANES, LANES)
    idx_p, dst_p, g_p, offs_p, cnt_p = _plan(aff_p, NP, CP, 0, 0, "plan_prompt")
    idx_s, dst_s, g_s, offs_s, cnt_s = _plan(aff_s, NS, CS, NP, 2 * NP, "plan_sample")
    idx = jnp.concatenate([idx_p, idx_s], axis=1).reshape(E * CT)
    dst = jnp.concatenate([dst_p, dst_s], axis=1).reshape(E * CT)
    g_col = jnp.concatenate([g_p, g_s], axis=1).reshape(E, CT, 1)
    z = _experts(idx, dst, h2_all, g_col, w_gate, w_up, w_down, layer)
    lo = jnp.concatenate([offs_p.reshape(NP), offs_s.reshape(NS) + 2.0 * NP])
    hi = lo + jnp.concatenate([cnt_p.reshape(NP), cnt_s.reshape(NS)])
    lo_t = lo.reshape(NT // TC, TC)[:, 0]
    hi_t = hi.reshape(NT // TC, TC)[:, TC - 1]
    rs = lo_t.astype(I32)
    rn = (hi_t - lo_t).astype(I32)
    return _combine(rs, rn, z, lo.reshape(NT, 1), hi.reshape(NT, 1), x1_all, mod_l, lng, lnb, split_out)


def _rms_heads(x, g, nheads):
    outs = []
    for h in range(nheads):
        xh = x[:, h * HEAD_DIM:(h + 1) * HEAD_DIM]
        outs.append(xh * lax.rsqrt(jnp.mean(xh * xh, axis=-1, keepdims=True) + EPS) * g)
    return outs


def _rope(x, cos, sin):
    lane = lax.broadcasted_iota(I32, x.shape, 1)
    first_half = (lane % ROPE_AXIS_DIM) < ROPE_AXIS_DIM // 2
    partner = jnp.where(first_half, pltpu.roll(x, LANES - ROPE_AXIS_DIM // 2, 1), pltpu.roll(x, ROPE_AXIS_DIM // 2, 1))
    return x * cos + partner * sin


def _qkv_kernel(x_ref, mod_ref, w_ref, gq_ref, gk_ref, cos_ref, sin_ref, q_ref, k_ref, v_ref, kc_ref, vc_ref):
    i = pl.program_id(0)
    sh1 = mod_ref[0:1, :]
    sc1 = mod_ref[1:2, :]
    hb = _bf(x_ref[...] * (1.0 + sc1) + sh1)
    qkv = _dot(hb, w_ref[...])
    qs = _rms_heads(qkv[:, :ATTN_QD], gq_ref[...], ATTN_HEADS)
    ks = _rms_heads(qkv[:, ATTN_QD:ATTN_QD + ATTN_KVD], gk_ref[...], ATTN_KV)
    v = qkv[:, ATTN_QD + ATTN_KVD:]
    v_ref[...] = _bf(v)

    @pl.when(i < NP // TM)
    def _context():
        for h in range(ATTN_HEADS):
            q_ref[:, h * HEAD_DIM:(h + 1) * HEAD_DIM] = _bf(qs[h])
        for h in range(ATTN_KV):
            k_ref[:, h * HEAD_DIM:(h + 1) * HEAD_DIM] = _bf(ks[h])
            for s in range(TM // SEQ):
                kc_ref[s, h] = ks[h][s * SEQ:(s + 1) * SEQ, :]
                vc_ref[s, h] = v[s * SEQ:(s + 1) * SEQ, h * HEAD_DIM:(h + 1) * HEAD_DIM]

    @pl.when(i >= NP // TM)
    def _latent():
        cos = cos_ref[...]
        sin = sin_ref[...]
        for h in range(ATTN_HEADS):
            q_ref[:, h * HEAD_DIM:(h + 1) * HEAD_DIM] = _bf(_rope(qs[h], cos, sin))
        for h in range(ATTN_KV):
            k_ref[:, h * HEAD_DIM:(h + 1) * HEAD_DIM] = _bf(_rope(ks[h], cos, sin))


def _qkv(x_all, mod_l, w, gq, gk, cos_t, sin_t):
    tok = lambda n: pl.BlockSpec((TM, n), lambda i: (i, 0))
    pos_map = lambda i: (jnp.maximum(i - NP // TM, 0) % (DEC_SEQ // TM), 0)
    cache = pl.BlockSpec((TM // SEQ, None, ATTN_KV, SEQ, HEAD_DIM),
                         lambda i: (jnp.minimum(i, NP // TM - 1), 0, 0, 0, 0))
    cache_shape = jax.ShapeDtypeStruct((BATCH, 1, ATTN_KV, SEQ, HEAD_DIM), F32)
    return pl.pallas_call(
        _qkv_kernel,
        grid=(NT // TM,),
        in_specs=[
            tok(D),
            pl.BlockSpec((None, 6, D), lambda i: (_mod_row(i), 0, 0)),
            pl.BlockSpec((D, ATTN_QD + 2 * ATTN_KVD), lambda i: (0, 0)),
            pl.BlockSpec((1, HEAD_DIM), lambda i: (0, 0)),
            pl.BlockSpec((1, HEAD_DIM), lambda i: (0, 0)),
            pl.BlockSpec((TM, HEAD_DIM), pos_map),
            pl.BlockSpec((TM, HEAD_DIM), pos_map),
        ],
        out_specs=[tok(ATTN_QD), tok(ATTN_KVD), tok(ATTN_KVD), cache, cache],
        out_shape=[jax.ShapeDtypeStruct((NT, ATTN_QD), BF16), jax.ShapeDtypeStruct((NT, ATTN_KVD), BF16),
                   jax.ShapeDtypeStruct((NT, ATTN_KVD), BF16), cache_shape, cache_shape],
        compiler_params=_cparams(("arbitrary",)),
        name="qkv",
    )(x_all, mod_l, w, gq, gk, cos_t, sin_t)


def _softmax_pv(s_list, v_list):
    mx = s_list[0].max(axis=-1, keepdims=True)
    for s in s_list[1:]:
        mx = jnp.maximum(mx, s.max(axis=-1, keepdims=True))
    ps = [jnp.exp(s - mx) for s in s_list]
    den = ps[0].sum(axis=-1, keepdims=True)
    for p in ps[1:]:
        den = den + p.sum(axis=-1, keepdims=True)
    out = None
    for p, v in zip(ps, v_list):
        o = _dot(_bf(p / den), v)
        out = o if out is None else out + o
    return out


QB = 256


def _attn_prompt_kernel(q_ref, k_ref, v_ref, o_ref):
    kb = k_ref[...]
    vb = v_ref[...]
    for g in range(ATTN_G):
        qh = q_ref[:, g * HEAD_DIM:(g + 1) * HEAD_DIM]
        s = _dot_nt(qh, kb) * (HEAD_DIM ** -0.5)
        o_ref[:, g * HEAD_DIM:(g + 1) * HEAD_DIM] = _bf(_softmax_pv([s], [vb]))


def _attn_sample_kernel(q_ref, k_ref, v_ref, ck_ref, cv_ref, o_ref):
    kb = k_ref[...]
    vb = v_ref[...]
    ckb = _bf(ck_ref[...])
    cvb = _bf(cv_ref[...])
    for g in range(ATTN_G):
        for qb in range(DEC_SEQ // QB):
            qh = q_ref[qb * QB:(qb + 1) * QB, g * HEAD_DIM:(g + 1) * HEAD_DIM]
            s_ctx = _dot_nt(qh, ckb) * (HEAD_DIM ** -0.5)
            s_lat = _dot_nt(qh, kb) * (HEAD_DIM ** -0.5)
            o_ref[qb * QB:(qb + 1) * QB, g * HEAD_DIM:(g + 1) * HEAD_DIM] = _bf(
                _softmax_pv([s_ctx, s_lat], [cvb, vb]))


def _attention(q_all, k_all, v_all, ctx_k, ctx_v):
    gw = ATTN_G * HEAD_DIM
    o_prompt = pl.pallas_call(
        _attn_prompt_kernel,
        grid=(BATCH, ATTN_KV),
        in_specs=[
            pl.BlockSpec((SEQ, gw), lambda b, h: (b, h)),
            pl.BlockSpec((SEQ, HEAD_DIM), lambda b, h: (b, h)),
            pl.BlockSpec((SEQ, HEAD_DIM), lambda b, h: (b, h)),
        ],
        out_specs=pl.BlockSpec((SEQ, gw), lambda b, h: (b, h)),
        out_shape=jax.ShapeDtypeStruct((NP, ATTN_QD), BF16),
        compiler_params=_cparams(("arbitrary", "arbitrary")),
        name="attn_prompt",
    )(q_all, k_all, v_all)
    b0 = NP // DEC_SEQ
    o_sample = pl.pallas_call(
        _attn_sample_kernel,
        grid=(DEC_BATCH, ATTN_KV),
        in_specs=[
            pl.BlockSpec((DEC_SEQ, gw), lambda b, h: (b0 + b, h)),
            pl.BlockSpec((DEC_SEQ, HEAD_DIM), lambda b, h: (b0 + b, h)),
            pl.BlockSpec((DEC_SEQ, HEAD_DIM), lambda b, h: (b0 + b, h)),
            pl.BlockSpec((None, None, PAST_LEN, HEAD_DIM), lambda b, h: (b, h, 0, 0)),
            pl.BlockSpec((None, None, PAST_LEN, HEAD_DIM), lambda b, h: (b, h, 0, 0)),
        ],
        out_specs=pl.BlockSpec((DEC_SEQ, gw), lambda b, h: (b, h)),
        out_shape=jax.ShapeDtypeStruct((NS, ATTN_QD), BF16),
        compiler_params=_cparams(("arbitrary", "arbitrary")),
        name="attn_sample",
    )(q_all, k_all, v_all, ctx_k, ctx_v)
    return o_prompt, o_sample


def _rope_tables():
    pos = jnp.arange(DEC_SEQ)
    rowp = (pos // GRID_W).astype(F32)
    colp = (pos % GRID_W).astype(F32)
    inv = ROPE_THETA ** (-jnp.arange(0, ROPE_AXIS_DIM, 2, dtype=F32) / ROPE_AXIS_DIM)
    a_row = rowp[:, None] * inv
    a_col = colp[:, None] * inv
    cos_t = jnp.concatenate([jnp.cos(a_row), jnp.cos(a_row), jnp.cos(a_col), jnp.cos(a_col)], axis=1)
    sin_t = jnp.concatenate([-jnp.sin(a_row), jnp.sin(a_row), -jnp.sin(a_col), jnp.sin(a_col)], axis=1)
    return cos_t, sin_t


def kernel(x_prompt, x_sample, state_gla_fwd, state_gla_bwd, cache_attn_k, cache_attn_v, c, c_ctx, w_mod, b_mod, ln_g, ln_b, w_gla_in, w_gla_gf1, w_gla_gf2, b_gla_gf, w_gla_gb1, w_gla_gb2, b_gla_gb, g_gla_norm, w_gla_out, w_attn_in, g_attn_q, g_attn_k, w_attn_out, w_router, w_moe_gate, w_moe_up, w_moe_down):
    xp = x_prompt.reshape(NP, D)
    xs = x_sample.reshape(NS, D)
    cvec = jnp.concatenate([c_ctx[None, :], c, jnp.zeros((8 - NMOD, D), F32)], axis=0)
    mod = _adaln(cvec, w_mod, b_mod)[:, :NMOD].reshape(DEPTH, NMOD, 6, D)

    w1 = _bf(jnp.concatenate([w_gla_gf1[0], w_gla_gb1[0]], axis=1))
    zr = jnp.zeros((GLA_RANK, GLA_KD), F32)
    w2 = _bf(jnp.concatenate([jnp.concatenate([w_gla_gf2[0], zr], axis=1),
                              jnp.concatenate([zr, w_gla_gb2[0]], axis=1)], axis=0))
    b2 = jnp.concatenate([b_gla_gf[0], b_gla_gb[0]])[None, :]
    gn = g_gla_norm[0][None, :]
    win = _bf(w_gla_in[0])
    og_p, new_f, new_b = _gla_call(xp, mod[0], win, w1, w2, b2, gn, None, None, prompt=True)
    (og_s,) = _gla_call(xs, mod[0], win, w1, w2, b2, gn, state_gla_fwd[:, 0], state_gla_bwd[:, 0],
                        prompt=False)
    x1, h2, aff = _post_mixer(og_p, og_s, _bf(w_gla_out[0]), (xp, xs), mod[0], ln_g[0, 0][None, :],
                              ln_b[0, 0][None, :], w_router[0].T)
    x_all = _moe(h2, aff, x1, mod[0], ln_g[0, 1][None, :], ln_b[0, 1][None, :],
                 w_moe_gate, w_moe_up, w_moe_down, 0, False)

    cos_t, sin_t = _rope_tables()
    q_all, k_all, v_all, new_k, new_v = _qkv(x_all, mod[1], _bf(w_attn_in[0]), g_attn_q[0][None, :],
                                             g_attn_k[0][None, :], cos_t, sin_t)
    oa_p, oa_s = _attention(q_all, k_all, v_all, cache_attn_k[:, 0], cache_attn_v[:, 0])
    x1, h2, aff = _post_mixer(oa_p, oa_s, _bf(w_attn_out[0]), (x_all,), mod[1], ln_g[1, 0][None, :],
                              ln_b[1, 0][None, :], w_router[1].T)
    yp, ys = _moe(h2, aff, x1, mod[1], ln_g[1, 1][None, :], ln_b[1, 1][None, :],
                  w_moe_gate, w_moe_up, w_moe_down, 1, True)
    return (yp.reshape(BATCH, SEQ, D), ys.reshape(DEC_BATCH, DEC_SEQ, D), new_f, new_b, new_k, new_v)
```

```python
import functools
import math

import jax
import jax.numpy as jnp
from jax import lax
from jax.experimental import pallas as pl
from jax.experimental.pallas import tpu as pltpu

F32 = jnp.float32
BF16 = jnp.bfloat16
I32 = jnp.int32

D = 1024
BATCH, SEQ = 32, 256
DEC_BATCH, DEC_SEQ = 2, 1024
PAST_LEN = 256
DEPTH = 2
GRID_W = 64
NP = BATCH * SEQ
NS = DEC_BATCH * DEC_SEQ
NT = NP + NS
NMOD = 1 + DEC_BATCH
GLA_HEADS, GLA_DK, GLA_DV = 4, 128, 256
GLA_KD, GLA_VD = 512, 1024
GLA_RANK = 16
GLA_TAU = 16.0
CHUNK = 64
ATTN_HEADS, ATTN_KV, HEAD_DIM = 8, 2, 128
ATTN_G = ATTN_HEADS // ATTN_KV
ATTN_QD, ATTN_KVD = 1024, 256
ROPE_AXIS_DIM = 64
ROPE_THETA = 10000.0
E = 16
CP = 2 * NP // E
CS = 2 * NS // E
CT = CP + CS
FF = 2 * D
ALPHA = (2 * DEPTH) ** 0.25
EPS = 1e-6

LANES = 128
VMEM_LIMIT = 56 * 1024 * 1024
TM = 512
GLA_BLK = 1024
GLA_GROUP = 256
TF = 512
TC = 256
RK = 256
RK_SHIFT = 8
PF = 4
ZROWS = 2 * NT
NBLK = 128
SLAB = D // LANES


def _cparams(sem):
    return pltpu.CompilerParams(dimension_semantics=sem, vmem_limit_bytes=VMEM_LIMIT)


def _bf(x):
    return x.astype(BF16)


def _dot(a, b):
    return jnp.dot(a, b, preferred_element_type=F32)


def _dot_nt(a, b):
    return lax.dot_general(a, b, (((1,), (1,)), ((), ())), preferred_element_type=F32)


def _dot_tn(a, b):
    return lax.dot_general(a, b, (((0,), (0,)), ((), ())), preferred_element_type=F32)


def _layer_norm(y, g, b):
    mu = jnp.mean(y, axis=-1, keepdims=True)
    yc = y - mu
    var = jnp.mean(yc * yc, axis=-1, keepdims=True)
    return yc * lax.rsqrt(var + EPS) * g + b


def _silu(x):
    return x * jax.nn.sigmoid(x)


def _mod_row(i):
    return jnp.where(i < NP // TM, 0, 1 + (i - NP // TM) // (DEC_SEQ // TM))


ADA_TN = 1024


def _adaln_kernel(c_ref, w_ref, b_ref, o_ref):
    s = _silu(c_ref[...])
    o_ref[...] = _dot(_bf(s), _bf(w_ref[...])) + b_ref[...]


def _adaln(cvec, w_mod, b_mod):
    return pl.pallas_call(
        _adaln_kernel,
        grid=(DEPTH, 6 * D // ADA_TN),
        in_specs=[
            pl.BlockSpec((8, D), lambda l, n: (0, 0)),
            pl.BlockSpec((None, D, ADA_TN), lambda l, n: (l, 0, n)),
            pl.BlockSpec((None, 1, ADA_TN), lambda l, n: (l, 0, n)),
        ],
        out_specs=pl.BlockSpec((None, 8, ADA_TN), lambda l, n: (l, 0, n)),
        out_shape=jax.ShapeDtypeStruct((DEPTH, 8, 6 * D), F32),
        compiler_params=_cparams(("arbitrary", "arbitrary")),
        name="adaln",
    )(cvec, w_mod, b_mod.reshape(DEPTH, 1, 6 * D))


def _log_sigmoid(x):
    return jnp.minimum(x, 0.0) - jnp.log1p(jnp.exp(-jnp.abs(x)))


def _split3(x):
    p1 = _bf(x)
    r1 = x - p1.astype(F32)
    p2 = _bf(r1)
    p3 = _bf(r1 - p2.astype(F32))
    return p1, p2, p3


def _gla_kernel(nseq, t_seq, has_state, emit_state, *refs):
    it = iter(refs)
    x_ref, mod_ref, win_ref, w1_ref, w2_ref, b2_ref, gn_ref = (next(it) for _ in range(7))
    s0f_ref = s0b_ref = None
    if has_state:
        s0f_ref, s0b_ref = next(it), next(it)
    og_ref = next(it)
    sf_ref = sb_ref = None
    if emit_state:
        sf_ref, sb_ref = next(it), next(it)
    hb_scr, q_scr, k_scr, v_scr, r_scr, lgf_scr, lgb_scr, of_scr, ob_scr = (next(it) for _ in range(9))

    cpg = GLA_GROUP // CHUNK
    gps = t_seq // GLA_GROUP
    sh1 = mod_ref[0:1, :]
    sc1 = mod_ref[1:2, :]
    hb_scr[...] = _bf(x_ref[...] * (1.0 + sc1) + sh1)
    z1 = _bf(_dot(hb_scr[...], w1_ref[...]))

    row = lax.broadcasted_iota(I32, (GLA_GROUP, GLA_GROUP), 0)
    col = lax.broadcasted_iota(I32, (GLA_GROUP, GLA_GROUP), 1)
    sh = int(math.log2(CHUNK))
    same_chunk = lax.shift_right_logical(row, sh) == lax.shift_right_logical(col, sh)
    mask_f = same_chunk & (row >= col)
    mask_b = same_chunk & (row <= col)
    tri_f = _bf(mask_f.astype(F32))
    tri_b = _bf(mask_b.astype(F32))

    def group_scan(rows, forward, lg_scr, st):
        tri, mask = (tri_f, mask_f) if forward else (tri_b, mask_b)
        p1, p2, p3 = _split3(lg_scr[rows, :])
        g3 = _dot(tri, jnp.concatenate([p1, p2, p3], axis=1))
        g = g3[:, 0:LANES] + g3[:, LANES:2 * LANES] + g3[:, 2 * LANES:3 * LANES]
        last = CHUNK - 1 if forward else 0
        tot = [g[c * CHUNK + last:c * CHUNK + last + 1, :] for c in range(cpg)]
        g_tot = jnp.concatenate([jnp.broadcast_to(t, (CHUNK, GLA_DK)) for t in tot], axis=0)
        q = q_scr[rows, :]
        k = k_scr[rows, :]
        v = v_scr[rows, :]
        qg = _bf(q * jnp.exp(g))
        kg = _bf(k * jnp.exp(-g))
        kd = _bf(k * jnp.exp(g_tot - g))
        a = _bf(jnp.where(mask, _dot_nt(qg, kg), 0.0))
        o_intra = _dot(a, v)
        o_inter = [None] * cpg
        for c in (range(cpg) if forward else reversed(range(cpg))):
            sl = slice(c * CHUNK, (c + 1) * CHUNK)
            o_inter[c] = _dot_nt(qg[sl], _bf(st))
            st = jnp.exp(tot[c]) * st + _dot_tn(v[sl], kd[sl])
        return o_intra + jnp.concatenate(o_inter, axis=0), st

    for h in range(GLA_HEADS):
        hb = hb_scr[...]
        q_scr[...] = _dot(hb, win_ref[:, h * GLA_DK:(h + 1) * GLA_DK]) * (GLA_DK ** -0.5)
        k_scr[...] = _dot(hb, win_ref[:, GLA_KD + h * GLA_DK:GLA_KD + (h + 1) * GLA_DK])
        v_scr[...] = _bf(_dot(hb, win_ref[:, 2 * GLA_KD + h * GLA_DV:2 * GLA_KD + (h + 1) * GLA_DV]))
        r_scr[...] = _dot(hb, win_ref[:, 2 * GLA_KD + GLA_VD + h * GLA_DV:2 * GLA_KD + GLA_VD + (h + 1) * GLA_DV])
        zf = _dot(z1, w2_ref[:, h * GLA_DK:(h + 1) * GLA_DK]) + b2_ref[:, h * GLA_DK:(h + 1) * GLA_DK]
        zb = (_dot(z1, w2_ref[:, GLA_KD + h * GLA_DK:GLA_KD + (h + 1) * GLA_DK])
              + b2_ref[:, GLA_KD + h * GLA_DK:GLA_KD + (h + 1) * GLA_DK])
        lgf_scr[...] = _log_sigmoid(zf) / GLA_TAU
        lgb_scr[...] = _log_sigmoid(zb) / GLA_TAU
        zero = jnp.zeros((GLA_DV, GLA_DK), F32)
        for s in range(nseq):
            st_f = s0f_ref[h].T if has_state else zero
            st_b = s0b_ref[h].T if has_state else zero
            for j in range(gps):
                gf = s * gps + j
                gb = s * gps + (gps - 1 - j)
                rows_f = slice(gf * GLA_GROUP, (gf + 1) * GLA_GROUP)
                rows_b = slice(gb * GLA_GROUP, (gb + 1) * GLA_GROUP)
                of_scr[rows_f, :], st_f = group_scan(rows_f, True, lgf_scr, st_f)
                ob_scr[rows_b, :], st_b = group_scan(rows_b, False, lgb_scr, st_b)
            if emit_state:
                sf_ref[s, h] = st_f.T
                sb_ref[s, h] = st_b.T
        o = of_scr[...] + ob_scr[...]
        on = o * lax.rsqrt(jnp.mean(o * o, axis=-1, keepdims=True) + EPS) * gn_ref[:, h * GLA_DV:(h + 1) * GLA_DV]
        og_ref[:, h * GLA_DV:(h + 1) * GLA_DV] = _bf(on * _silu(r_scr[...]))


def _gla_call(x_stream, mod0, win, w1, w2, b2, gn, state_f, state_b, *, prompt):
    nseq = GLA_BLK // SEQ if prompt else GLA_BLK // DEC_SEQ
    t_seq = SEQ if prompt else DEC_SEQ
    nblk = NP // GLA_BLK if prompt else NS // GLA_BLK
    has_state = not prompt
    emit_state = prompt
    if prompt:
        mod_map = lambda i: (0, 0, 0)
    else:
        mod_map = lambda i: (1 + i, 0, 0)
    in_specs = [
        pl.BlockSpec((GLA_BLK, D), lambda i: (i, 0)),
        pl.BlockSpec((None, 6, D), mod_map),
        pl.BlockSpec((D, 2 * GLA_KD + 2 * GLA_VD), lambda i: (0, 0)),
        pl.BlockSpec((D, 2 * GLA_RANK), lambda i: (0, 0)),
        pl.BlockSpec((2 * GLA_RANK, 2 * GLA_KD), lambda i: (0, 0)),
        pl.BlockSpec((1, 2 * GLA_KD), lambda i: (0, 0)),
        pl.BlockSpec((1, GLA_VD), lambda i: (0, 0)),
    ]
    args = [x_stream, mod0, win, w1, w2, b2, gn]
    if has_state:
        in_specs += [pl.BlockSpec((None, GLA_HEADS, GLA_DK, GLA_DV), lambda i: (i, 0, 0, 0))] * 2
        args += [state_f, state_b]
    out_specs = [pl.BlockSpec((GLA_BLK, GLA_VD), lambda i: (i, 0))]
    out_shape = [jax.ShapeDtypeStruct((nblk * GLA_BLK, GLA_VD), BF16)]
    if emit_state:
        st_spec = pl.BlockSpec((nseq, None, GLA_HEADS, GLA_DK, GLA_DV), lambda i: (i, 0, 0, 0, 0))
        out_specs += [st_spec, st_spec]
        out_shape += [jax.ShapeDtypeStruct((BATCH, 1, GLA_HEADS, GLA_DK, GLA_DV), F32)] * 2
    scratch = [
        pltpu.VMEM((GLA_BLK, D), BF16),
        pltpu.VMEM((GLA_BLK, GLA_DK), F32),
        pltpu.VMEM((GLA_BLK, GLA_DK), F32),
        pltpu.VMEM((GLA_BLK, GLA_DV), BF16),
        pltpu.VMEM((GLA_BLK, GLA_DV), F32),
        pltpu.VMEM((GLA_BLK, GLA_DK), F32),
        pltpu.VMEM((GLA_BLK, GLA_DK), F32),
        pltpu.VMEM((GLA_BLK, GLA_DV), F32),
        pltpu.VMEM((GLA_BLK, GLA_DV), F32),
    ]
    return pl.pallas_call(
        functools.partial(_gla_kernel, nseq, t_seq, has_state, emit_state),
        grid=(nblk,),
        in_specs=in_specs,
        out_specs=out_specs,
        out_shape=out_shape,
        scratch_shapes=scratch,
        compiler_params=_cparams(("arbitrary",)),
        name="gla_prompt" if prompt else "gla_sample",
    )(*args)


def _post_mixer_kernel(split_x, *refs):
    it = iter(refs)
    op_ref, os_ref, wout_ref = next(it), next(it), next(it)
    is_prompt = pl.program_id(0) < NP // TM
    if split_x:
        xp_ref, xs_ref = next(it), next(it)
        x = jnp.where(is_prompt, xp_ref[...], xs_ref[...])
    else:
        x = next(it)[...]
    mod_ref, lng_ref, lnb_ref, wr_ref, x1_ref, h2_ref, aff_ref = (next(it) for _ in range(7))
    ga1 = mod_ref[2:3, :]
    sh2 = mod_ref[3:4, :]
    sc2 = mod_ref[4:5, :]
    o = jnp.where(is_prompt, op_ref[...], os_ref[...])
    m = _dot(o, wout_ref[...])
    x1 = _layer_norm(ALPHA * x + ga1 * m, lng_ref[...], lnb_ref[...])
    x1_ref[...] = x1
    h2 = x1 * (1.0 + sc2) + sh2
    for s in range(SLAB):
        h2_ref[pl.ds(s, TM, stride=SLAB), :] = h2[:, s * LANES:(s + 1) * LANES]
    wr = wr_ref[...]
    a1 = _bf(wr)
    a2 = _bf(wr - a1.astype(F32))
    b1 = _bf(h2)
    b2 = _bf(h2 - b1.astype(F32))
    lt = _dot_nt(a1, b1) + (_dot_nt(a1, b2) + _dot_nt(a2, b1))
    ex = jnp.exp(lt - jnp.max(lt, axis=0, keepdims=True))
    aff_ref[...] = ex / jnp.sum(ex, axis=0, keepdims=True)


def _post_mixer(o_p, o_s, wout, xs, mod_l, lng, lnb, wr_t):
    tok = pl.BlockSpec((TM, D), lambda i: (i, 0))
    tok_p = pl.BlockSpec((TM, D), lambda i: (jnp.minimum(i, NP // TM - 1), 0))
    tok_s = pl.BlockSpec((TM, D), lambda i: (jnp.maximum(i - NP // TM, 0), 0))
    split_x = len(xs) == 2
    return pl.pallas_call(
        functools.partial(_post_mixer_kernel, split_x),
        grid=(NT // TM,),
        in_specs=[
            tok_p,
            tok_s,
            pl.BlockSpec((D, D), lambda i: (0, 0)),
            *([tok_p, tok_s] if split_x else [tok]),
            pl.BlockSpec((None, 6, D), lambda i: (_mod_row(i), 0, 0)),
            pl.BlockSpec((1, D), lambda i: (0, 0)),
            pl.BlockSpec((1, D), lambda i: (0, 0)),
            pl.BlockSpec((E, D), lambda i: (0, 0)),
        ],
        out_specs=[tok, pl.BlockSpec((TM * SLAB, LANES), lambda i: (i, 0)), pl.BlockSpec((E, TM), lambda i: (0, i))],
        out_shape=[jax.ShapeDtypeStruct((NT, D), F32), jax.ShapeDtypeStruct((NT * SLAB, LANES), F32),
                   jax.ShapeDtypeStruct((E, NT), F32)],
        compiler_params=_cparams(("arbitrary",)),
        name="post_mixer",
    )(o_p, o_s, wout, *xs, mod_l, lng, lnb, wr_t)


def _plan_kernel(n_tok, cap, tok_off, row_off, aff_ref, idx_ref, dst_ref, g_ref, offs_ref, cnt_ref,
                 sel_scr, csl_scr, cs_scr, rhi_scr, rlo_scr, g1_scr, g2_scr, g3_scr):
    nb = n_tok // LANES
    rows = E * nb
    aff = aff_ref[...]

    thr_bits = jnp.zeros((E, 1, 1), I32)
    for b in range(30, -1, -1):
        cand = thr_bits | (1 << b)
        cand_f = lax.bitcast_convert_type(cand, F32)
        cnt = jnp.sum(jnp.sum((aff >= cand_f).astype(F32), axis=2, keepdims=True), axis=1, keepdims=True)
        thr_bits = jnp.where(cnt >= cap, cand, thr_bits)
    thr = lax.bitcast_convert_type(thr_bits, F32)

    triu = _bf((lax.broadcasted_iota(I32, (LANES, LANES), 0)
                <= lax.broadcasted_iota(I32, (LANES, LANES), 1)).astype(F32))
    rr = lax.broadcasted_iota(I32, (rows, rows), 0)
    cc = lax.broadcasted_iota(I32, (rows, rows), 1)
    sh = int(math.log2(nb))
    same = lax.shift_right_logical(rr, sh) == lax.shift_right_logical(cc, sh)
    blk = _bf((same & (cc < rr)).astype(F32))

    def cumsum_tokens(m2):
        local = _dot(_bf(m2), triu)
        tot = jnp.broadcast_to(local[:, LANES - 1:LANES], (rows, LANES))
        base = _dot(blk, _bf(tot))
        return local, local + base

    gt = aff > thr
    eq = aff == thr
    n_gt = jnp.sum(jnp.sum(gt.astype(F32), axis=2, keepdims=True), axis=1, keepdims=True)
    need = jnp.broadcast_to(cap - n_gt, (E, nb, LANES)).reshape(rows, LANES)
    eq2 = eq.astype(F32).reshape(rows, LANES)
    gt2 = gt.astype(F32).reshape(rows, LANES)
    _, eq_incl = cumsum_tokens(eq2)
    sel2 = jnp.maximum(gt2, eq2 * ((eq_incl - eq2) < need).astype(F32))
    csl2, cs2 = cumsum_tokens(sel2)
    sel3 = sel2.reshape(E, nb, LANES)
    pos3 = (cs2 - sel2).reshape(E, nb, LANES)

    offs = jnp.sum(pos3, axis=0)
    cnt_ref[...] = jnp.sum(sel3, axis=0)
    offs_ref[...] = offs
    within = jnp.zeros((nb, LANES), F32)
    for e in range(E):
        r = offs + within + float(row_off)
        rhi = jnp.floor(r * (1.0 / LANES))
        rhi_scr[e, 0:nb, :] = rhi
        rlo_scr[e, 0:nb, :] = r - rhi * LANES
        within = within + sel3[e]
    sel_scr[:, 0:nb, :] = sel3
    csl_scr[:, 0:nb, :] = csl2.reshape(E, nb, LANES)
    cs_scr[:, 0:nb, :] = cs2.reshape(E, nb, LANES)
    p1, p2, p3 = _split3(aff)
    g1_scr[:, 0:nb, :] = p1.astype(F32)
    g2_scr[:, 0:nb, :] = p2.astype(F32)
    g3_scr[:, 0:nb, :] = p3.astype(F32)
    if nb < NBLK:
        zpad = jnp.zeros((E, NBLK - nb, LANES), F32)
        for scr in (sel_scr, csl_scr, rhi_scr, rlo_scr, g1_scr, g2_scr, g3_scr):
            scr[:, nb:NBLK, :] = zpad
        cs_scr[:, nb:NBLK, :] = jnp.full((E, NBLK - nb, LANES), 4.0 * cap, F32)

    jrow = lax.broadcasted_iota(I32, (1, cap), 1).astype(F32)
    sub = lax.broadcasted_iota(I32, (NBLK, cap), 0).astype(F32)

    def per_expert(e, carry):
        cs_e = cs_scr[e]
        csl_e = csl_scr[e]
        cs_end = cs_e[:, LANES - 1:LANES]
        base_e = cs_end - csl_e[:, LANES - 1:LANES]
        bidx = jnp.sum((cs_end <= jrow).astype(F32), axis=0, keepdims=True)
        ohb = sub == bidx
        jl = jrow - jnp.sum(jnp.where(ohb, base_e, 0.0), axis=0, keepdims=True)
        ohb_bf = _bf(ohb.astype(F32))
        crow = _dot(_bf(csl_e.T), ohb_bf)
        lidx = jnp.sum((crow <= jl).astype(F32), axis=0, keepdims=True)
        ohl = sub == lidx

        def pick(scr):
            prow = _dot(_bf(scr[e].T), ohb_bf)
            return jnp.sum(jnp.where(ohl, prow, 0.0), axis=0, keepdims=True)

        idx = bidx * LANES + lidx + float(tok_off)
        dst = pick(rhi_scr) * LANES + pick(rlo_scr)
        gate = pick(g1_scr) + pick(g2_scr) + pick(g3_scr)
        idx_ref[pl.ds(e, 1), :] = idx.astype(I32)
        dst_ref[pl.ds(e, 1), :] = dst.astype(I32)
        g_ref[pl.ds(e, 1), :] = gate
        return carry

    lax.fori_loop(0, E, per_expert, 0)


def _plan(aff3, n_tok, cap, tok_off, row_off, name):
    nb = n_tok // LANES
    scr = [pltpu.VMEM((E, NBLK, LANES), F32) for _ in range(8)]
    return pl.pallas_call(
        functools.partial(_plan_kernel, n_tok, cap, tok_off, row_off),
        out_shape=[jax.ShapeDtypeStruct((E, cap), I32), jax.ShapeDtypeStruct((E, cap), I32),
                   jax.ShapeDtypeStruct((E, cap), F32), jax.ShapeDtypeStruct((nb, LANES), F32),
                   jax.ShapeDtypeStruct((nb, LANES), F32)],
        scratch_shapes=scr,
        compiler_params=pltpu.CompilerParams(vmem_limit_bytes=VMEM_LIMIT),
        name=name,
    )(aff3)


DMA_UNROLL = 8


def _expert_kernel(idx_ref, dst_ref, h_hbm, g_ref, wg_ref, wu_ref, wd_ref, z_hbm,
                   xe_scr, xb_scr, acc_scr, ye_scr, gsem, ssem):
    e = pl.program_id(0)
    f = pl.program_id(1)
    ne = pl.num_programs(0)
    nf = pl.num_programs(1)

    def slab(r):
        return pl.ds(pl.multiple_of(r * SLAB, SLAB), SLAB)

    def start_gather(ex, slot):
        def start(j, c):
            pltpu.make_async_copy(h_hbm.at[slab(idx_ref[ex * CT + j]), :],
                                  xe_scr.at[slot, slab(j), :], gsem.at[slot]).start()
            return c

        lax.fori_loop(0, CT, start, 0, unroll=DMA_UNROLL)

    def wait_gather(slot):
        pltpu.make_async_copy(h_hbm.at[pl.ds(0, CT * SLAB), :], xe_scr.at[slot], gsem.at[slot]).wait()

    def wait_scatter():
        pltpu.make_async_copy(ye_scr, z_hbm.at[pl.ds(0, CT * SLAB), :], ssem).wait()

    @pl.when((f == 0) & (e == 0))
    def _first_gather():
        start_gather(0, 0)

    @pl.when(f == 0)
    def _rows_ready():
        slot = e % 2
        wait_gather(slot)
        xe = xe_scr.at[slot]
        for s in range(SLAB):
            xb_scr[:, s * LANES:(s + 1) * LANES] = _bf(xe[pl.ds(s, CT, stride=SLAB), :])
        acc_scr[...] = jnp.zeros((CT, D), F32)

        @pl.when(e + 1 < ne)
        def _prefetch():
            start_gather(e + 1, 1 - slot)

    xb = xb_scr[...]
    hg = _dot(xb, _bf(wg_ref[...]))
    hu = _dot(xb, _bf(wu_ref[...]))
    hid = _bf(_silu(hg) * hu)
    acc_scr[...] += _dot(hid, _bf(wd_ref[...]))

    @pl.when(f == nf - 1)
    def _scatter():
        @pl.when(e > 0)
        def _staging_free():
            wait_scatter()

        g = g_ref[...]
        for s in range(SLAB):
            ye_scr[pl.ds(s, CT, stride=SLAB), :] = acc_scr[:, s * LANES:(s + 1) * LANES] * g

        def start(j, c):
            pltpu.make_async_copy(ye_scr.at[slab(j), :], z_hbm.at[slab(dst_ref[e * CT + j]), :], ssem).start()
            return c

        lax.fori_loop(0, CT, start, 0, unroll=DMA_UNROLL)

        @pl.when(e == ne - 1)
        def _drain():
            wait_scatter()


def _experts(idx_flat, dst_flat, h_all, g_col, w_gate, w_up, w_down, layer):
    grid_spec = pltpu.PrefetchScalarGridSpec(
        num_scalar_prefetch=2,
        grid=(E, FF // TF),
        in_specs=[
            pl.BlockSpec(memory_space=pl.ANY),
            pl.BlockSpec((None, CT, 1), lambda e, f, *_: (e, 0, 0)),
            pl.BlockSpec((None, None, D, TF), lambda e, f, *_: (layer, e, 0, f)),
            pl.BlockSpec((None, None, D, TF), lambda e, f, *_: (layer, e, 0, f)),
            pl.BlockSpec((None, None, TF, D), lambda e, f, *_: (layer, e, f, 0)),
        ],
        out_specs=pl.BlockSpec(memory_space=pl.ANY),
        scratch_shapes=[
            pltpu.VMEM((2, CT * SLAB, LANES), F32),
            pltpu.VMEM((CT, D), BF16),
            pltpu.VMEM((CT, D), F32),
            pltpu.VMEM((CT * SLAB, LANES), F32),
            pltpu.SemaphoreType.DMA((2,)),
            pltpu.SemaphoreType.DMA,
        ],
    )
    return pl.pallas_call(
        _expert_kernel,
        grid_spec=grid_spec,
        out_shape=jax.ShapeDtypeStruct((ZROWS * SLAB, LANES), F32),
        compiler_params=_cparams(("arbitrary", "arbitrary")),
        name="experts",
    )(idx_flat, dst_flat, h_all, g_col, w_gate, w_up, w_down)


def _combine_kernel(split_out, rs_ref, rn_ref, z_hbm, lo_ref, hi_ref, x1_ref, mod_ref, lng_ref, lnb_ref, *refs):
    if split_out:
        xp_ref, xs_ref, zbuf, acc_scr, sem = refs
    else:
        x2_ref, zbuf, acc_scr, sem = refs
    i = pl.program_id(0)
    nt = pl.num_programs(0)

    def first_row(t):
        return lax.shift_left(lax.shift_right_logical(rs_ref[t], 3), 3)

    def num_chunks(t):
        return lax.shift_right_logical(rs_ref[t] + rn_ref[t] - first_row(t) + (RK - 1), RK_SHIFT)

    def chunk_copy(t, c, bank, slot):
        start = jnp.minimum(first_row(t) + c * RK, ZROWS - RK)
        rows = pl.ds(pl.multiple_of(start * SLAB, 8 * SLAB), RK * SLAB)
        return pltpu.make_async_copy(z_hbm.at[rows, :], zbuf.at[bank, slot], sem.at[bank, slot])

    def prefetch(t, bank):
        n = num_chunks(t)
        for c in range(PF):
            @pl.when(c < n)
            def _start():
                chunk_copy(t, c, bank, c).start()

    cur = i % 2

    @pl.when(i == 0)
    def _first_tile():
        prefetch(0, 0)

    @pl.when(i + 1 < nt)
    def _next_tile():
        prefetch(jnp.minimum(i + 1, nt - 1), 1 - cur)

    s8 = first_row(i)
    nchunk = num_chunks(i)
    lo = lo_ref[...]
    hi = hi_ref[...]
    acc_scr[...] = jnp.zeros((TC, D), F32)

    def accumulate(c, zs):
        want = s8 + c * RK
        start = jnp.minimum(want, ZROWS - RK)
        rowid = (start + lax.broadcasted_iota(I32, (1, RK), 1)).astype(F32)
        oh = (rowid >= lo) & (rowid < hi) & (rowid >= jnp.asarray(want, F32))
        oh = _bf(oh.astype(F32))
        z = jnp.concatenate([zs[pl.ds(s, RK, stride=SLAB), :] for s in range(SLAB)], axis=1)
        z1 = _bf(z)
        z2 = _bf(z - z1.astype(F32))
        acc_scr[...] += _dot(oh, z1) + _dot(oh, z2)

    for c in range(PF):
        @pl.when(c < nchunk)
        def _prefetched():
            chunk_copy(i, c, cur, c).wait()
            accumulate(c, zbuf.at[cur, c])

    def extra(c, carry):
        cp = chunk_copy(i, c, cur, 0)
        cp.start()
        cp.wait()
        accumulate(c, zbuf.at[cur, 0])
        return carry

    lax.fori_loop(PF, jnp.maximum(nchunk, PF), extra, 0)

    ga2 = mod_ref[5:6, :]
    x2 = _layer_norm(ALPHA * x1_ref[...] + ga2 * acc_scr[...], lng_ref[...], lnb_ref[...])
    if split_out:
        @pl.when(i < NP // TC)
        def _prompt():
            xp_ref[...] = x2

        @pl.when(i >= NP // TC)
        def _sample():
            xs_ref[...] = x2
    else:
        x2_ref[...] = x2


def _combine(rs, rn, z, lo_col, hi_col, x1_all, mod_l, lng, lnb, split_out):
    ratio = TM // TC
    if split_out:
        out_specs = [pl.BlockSpec((TC, D), lambda i, *_: (jnp.minimum(i, NP // TC - 1), 0)),
                     pl.BlockSpec((TC, D), lambda i, *_: (jnp.maximum(i - NP // TC, 0), 0))]
        out_shape = [jax.ShapeDtypeStruct((NP, D), F32), jax.ShapeDtypeStruct((NS, D), F32)]
    else:
        out_specs = pl.BlockSpec((TC, D), lambda i, *_: (i, 0))
        out_shape = jax.ShapeDtypeStruct((NT, D), F32)
    grid_spec = pltpu.PrefetchScalarGridSpec(
        num_scalar_prefetch=2,
        grid=(NT // TC,),
        in_specs=[
            pl.BlockSpec(memory_space=pl.ANY),
            pl.BlockSpec((TC, 1), lambda i, *_: (i, 0)),
            pl.BlockSpec((TC, 1), lambda i, *_: (i, 0)),
            pl.BlockSpec((TC, D), lambda i, *_: (i, 0)),
            pl.BlockSpec((None, 6, D), lambda i, *_: (_mod_row(i // ratio), 0, 0)),
            pl.BlockSpec((1, D), lambda i, *_: (0, 0)),
            pl.BlockSpec((1, D), lambda i, *_: (0, 0)),
        ],
        out_specs=out_specs,
        scratch_shapes=[pltpu.VMEM((2, PF, RK * SLAB, LANES), F32), pltpu.VMEM((TC, D), F32),
                        pltpu.SemaphoreType.DMA((2, PF))],
    )
    return pl.pallas_call(
        functools.partial(_combine_kernel, split_out),
        grid_spec=grid_spec,
        out_shape=out_shape,
        compiler_params=_cparams(("arbitrary",)),
        name="combine",
    )(rs, rn, z, lo_col, hi_col, x1_all, mod_l, lng, lnb)


def _moe(h2_all, aff_t, x1_all, mod_l, lng, lnb, w_gate, w_up, w_down, layer, split_out):
    aff_p = aff_t[:, :NP].reshape(E, NP // LANES, LANES)
    aff_s = aff_t[:, NP:].reshape(E, NS // LANES, LANES)
    idx_p, dst_p, g_p, offs_p, cnt_p = _plan(aff_p, NP, CP, 0, 0, "plan_prompt")
    idx_s, dst_s, g_s, offs_s, cnt_s = _plan(aff_s, NS, CS, NP, 2 * NP, "plan_sample")
    idx = jnp.concatenate([idx_p, idx_s], axis=1).reshape(E * CT)
    dst = jnp.concatenate([dst_p, dst_s], axis=1).reshape(E * CT)
    g_col = jnp.concatenate([g_p, g_s], axis=1).reshape(E, CT, 1)
    z = _experts(idx, dst, h2_all, g_col, w_gate, w_up, w_down, layer)
    lo = jnp.concatenate([offs_p.reshape(NP), offs_s.reshape(NS) + 2.0 * NP])
    hi = lo + jnp.concatenate([cnt_p.reshape(NP), cnt_s.reshape(NS)])
    lo_t = lo.reshape(NT // TC, TC)[:, 0]
    hi_t = hi.reshape(NT // TC, TC)[:, TC - 1]
    rs = lo_t.astype(I32)
    rn = (hi_t - lo_t).astype(I32)
    return _combine(rs, rn, z, lo.reshape(NT, 1), hi.reshape(NT, 1), x1_all, mod_l, lng, lnb, split_out)


def _rms_heads(x, g, nheads):
    outs = []
    for h in range(nheads):
        xh = x[:, h * HEAD_DIM:(h + 1) * HEAD_DIM]
        outs.append(xh * lax.rsqrt(jnp.mean(xh * xh, axis=-1, keepdims=True) + EPS) * g)
    return outs


def _rope(x, cos, sin):
    lane = lax.broadcasted_iota(I32, x.shape, 1)
    first_half = (lane % ROPE_AXIS_DIM) < ROPE_AXIS_DIM // 2
    partner = jnp.where(first_half, pltpu.roll(x, LANES - ROPE_AXIS_DIM // 2, 1), pltpu.roll(x, ROPE_AXIS_DIM // 2, 1))
    return x * cos + partner * sin


def _qkv_kernel(x_ref, mod_ref, w_ref, gq_ref, gk_ref, cos_ref, sin_ref, q_ref, k_ref, v_ref, kc_ref, vc_ref):
    i = pl.program_id(0)
    sh1 = mod_ref[0:1, :]
    sc1 = mod_ref[1:2, :]
    hb = _bf(x_ref[...] * (1.0 + sc1) + sh1)
    qkv = _dot(hb, w_ref[...])
    qs = _rms_heads(qkv[:, :ATTN_QD], gq_ref[...], ATTN_HEADS)
    ks = _rms_heads(qkv[:, ATTN_QD:ATTN_QD + ATTN_KVD], gk_ref[...], ATTN_KV)
    v = qkv[:, ATTN_QD + ATTN_KVD:]
    v_ref[...] = _bf(v)

    @pl.when(i < NP // TM)
    def _context():
        for h in range(ATTN_HEADS):
            q_ref[:, h * HEAD_DIM:(h + 1) * HEAD_DIM] = _bf(qs[h])
        for h in range(ATTN_KV):
            k_ref[:, h * HEAD_DIM:(h + 1) * HEAD_DIM] = _bf(ks[h])
            for s in range(TM // SEQ):
                kc_ref[s, h] = ks[h][s * SEQ:(s + 1) * SEQ, :]
                vc_ref[s, h] = v[s * SEQ:(s + 1) * SEQ, h * HEAD_DIM:(h + 1) * HEAD_DIM]

    @pl.when(i >= NP // TM)
    def _latent():
        cos = cos_ref[...]
        sin = sin_ref[...]
        for h in range(ATTN_HEADS):
            q_ref[:, h * HEAD_DIM:(h + 1) * HEAD_DIM] = _bf(_rope(qs[h], cos, sin))
        for h in range(ATTN_KV):
            k_ref[:, h * HEAD_DIM:(h + 1) * HEAD_DIM] = _bf(_rope(ks[h], cos, sin))


def _qkv(x_all, mod_l, w, gq, gk, cos_t, sin_t):
    tok = lambda n: pl.BlockSpec((TM, n), lambda i: (i, 0))
    pos_map = lambda i: (jnp.maximum(i - NP // TM, 0) % (DEC_SEQ // TM), 0)
    cache = pl.BlockSpec((TM // SEQ, None, ATTN_KV, SEQ, HEAD_DIM),
                         lambda i: (jnp.minimum(i, NP // TM - 1), 0, 0, 0, 0))
    cache_shape = jax.ShapeDtypeStruct((BATCH, 1, ATTN_KV, SEQ, HEAD_DIM), F32)
    return pl.pallas_call(
        _qkv_kernel,
        grid=(NT // TM,),
        in_specs=[
            tok(D),
            pl.BlockSpec((None, 6, D), lambda i: (_mod_row(i), 0, 0)),
            pl.BlockSpec((D, ATTN_QD + 2 * ATTN_KVD), lambda i: (0, 0)),
            pl.BlockSpec((1, HEAD_DIM), lambda i: (0, 0)),
            pl.BlockSpec((1, HEAD_DIM), lambda i: (0, 0)),
            pl.BlockSpec((TM, HEAD_DIM), pos_map),
            pl.BlockSpec((TM, HEAD_DIM), pos_map),
        ],
        out_specs=[tok(ATTN_QD), tok(ATTN_KVD), tok(ATTN_KVD), cache, cache],
        out_shape=[jax.ShapeDtypeStruct((NT, ATTN_QD), BF16), jax.ShapeDtypeStruct((NT, ATTN_KVD), BF16),
                   jax.ShapeDtypeStruct((NT, ATTN_KVD), BF16), cache_shape, cache_shape],
        compiler_params=_cparams(("arbitrary",)),
        name="qkv",
    )(x_all, mod_l, w, gq, gk, cos_t, sin_t)


def _softmax_pv(s_list, v_list):
    mx = s_list[0].max(axis=-1, keepdims=True)
    for s in s_list[1:]:
        mx = jnp.maximum(mx, s.max(axis=-1, keepdims=True))
    ps = [jnp.exp(s - mx) for s in s_list]
    den = ps[0].sum(axis=-1, keepdims=True)
    for p in ps[1:]:
        den = den + p.sum(axis=-1, keepdims=True)
    out = None
    for p, v in zip(ps, v_list):
        o = _dot(_bf(p / den), v)
        out = o if out is None else out + o
    return out


QB = 256


def _attn_prompt_kernel(q_ref, k_ref, v_ref, o_ref):
    kb = k_ref[...]
    vb = v_ref[...]
    for g in range(ATTN_G):
        qh = q_ref[:, g * HEAD_DIM:(g + 1) * HEAD_DIM]
        s = _dot_nt(qh, kb) * (HEAD_DIM ** -0.5)
        o_ref[:, g * HEAD_DIM:(g + 1) * HEAD_DIM] = _bf(_softmax_pv([s], [vb]))


def _attn_sample_kernel(q_ref, k_ref, v_ref, ck_ref, cv_ref, o_ref):
    kb = k_ref[...]
    vb = v_ref[...]
    ckb = _bf(ck_ref[...])
    cvb = _bf(cv_ref[...])
    for g in range(ATTN_G):
        for qb in range(DEC_SEQ // QB):
            qh = q_ref[qb * QB:(qb + 1) * QB, g * HEAD_DIM:(g + 1) * HEAD_DIM]
            s_ctx = _dot_nt(qh, ckb) * (HEAD_DIM ** -0.5)
            s_lat = _dot_nt(qh, kb) * (HEAD_DIM ** -0.5)
            o_ref[qb * QB:(qb + 1) * QB, g * HEAD_DIM:(g + 1) * HEAD_DIM] = _bf(
                _softmax_pv([s_ctx, s_lat], [cvb, vb]))


def _attention(q_all, k_all, v_all, ctx_k, ctx_v):
    gw = ATTN_G * HEAD_DIM
    o_prompt = pl.pallas_call(
        _attn_prompt_kernel,
        grid=(BATCH, ATTN_KV),
        in_specs=[
            pl.BlockSpec((SEQ, gw), lambda b, h: (b, h)),
            pl.BlockSpec((SEQ, HEAD_DIM), lambda b, h: (b, h)),
            pl.BlockSpec((SEQ, HEAD_DIM), lambda b, h: (b, h)),
        ],
        out_specs=pl.BlockSpec((SEQ, gw), lambda b, h: (b, h)),
        out_shape=jax.ShapeDtypeStruct((NP, ATTN_QD), BF16),
        compiler_params=_cparams(("arbitrary", "arbitrary")),
        name="attn_prompt",
    )(q_all, k_all, v_all)
    b0 = NP // DEC_SEQ
    o_sample = pl.pallas_call(
        _attn_sample_kernel,
        grid=(DEC_BATCH, ATTN_KV),
        in_specs=[
            pl.BlockSpec((DEC_SEQ, gw), lambda b, h: (b0 + b, h)),
            pl.BlockSpec((DEC_SEQ, HEAD_DIM), lambda b, h: (b0 + b, h)),
            pl.BlockSpec((DEC_SEQ, HEAD_DIM), lambda b, h: (b0 + b, h)),
            pl.BlockSpec((None, None, PAST_LEN, HEAD_DIM), lambda b, h: (b, h, 0, 0)),
            pl.BlockSpec((None, None, PAST_LEN, HEAD_DIM), lambda b, h: (b, h, 0, 0)),
        ],
        out_specs=pl.BlockSpec((DEC_SEQ, gw), lambda b, h: (b, h)),
        out_shape=jax.ShapeDtypeStruct((NS, ATTN_QD), BF16),
        compiler_params=_cparams(("arbitrary", "arbitrary")),
        name="attn_sample",
    )(q_all, k_all, v_all, ctx_k, ctx_v)
    return o_prompt, o_sample


def _rope_tables():
    pos = jnp.arange(DEC_SEQ)
    rowp = (pos // GRID_W).astype(F32)
    colp = (pos % GRID_W).astype(F32)
    inv = ROPE_THETA ** (-jnp.arange(0, ROPE_AXIS_DIM, 2, dtype=F32) / ROPE_AXIS_DIM)
    a_row = rowp[:, None] * inv
    a_col = colp[:, None] * inv
    cos_t = jnp.concatenate([jnp.cos(a_row), jnp.cos(a_row), jnp.cos(a_col), jnp.cos(a_col)], axis=1)
    sin_t = jnp.concatenate([-jnp.sin(a_row), jnp.sin(a_row), -jnp.sin(a_col), jnp.sin(a_col)], axis=1)
    return cos_t, sin_t


def kernel(x_prompt, x_sample, state_gla_fwd, state_gla_bwd, cache_attn_k, cache_attn_v, c, c_ctx, w_mod, b_mod, ln_g, ln_b, w_gla_in, w_gla_gf1, w_gla_gf2, b_gla_gf, w_gla_gb1, w_gla_gb2, b_gla_gb, g_gla_norm, w_gla_out, w_attn_in, g_attn_q, g_attn_k, w_attn_out, w_router, w_moe_gate, w_moe_up, w_moe_down):
    xp = x_prompt.reshape(NP, D)
    xs = x_sample.reshape(NS, D)
    cvec = jnp.concatenate([c_ctx[None, :], c, jnp.zeros((8 - NMOD, D), F32)], axis=0)
    mod = _adaln(cvec, w_mod, b_mod)[:, :NMOD].reshape(DEPTH, NMOD, 6, D)

    w1 = _bf(jnp.concatenate([w_gla_gf1[0], w_gla_gb1[0]], axis=1))
    zr = jnp.zeros((GLA_RANK, GLA_KD), F32)
    w2 = _bf(jnp.concatenate([jnp.concatenate([w_gla_gf2[0], zr], axis=1),
                              jnp.concatenate([zr, w_gla_gb2[0]], axis=1)], axis=0))
    b2 = jnp.concatenate([b_gla_gf[0], b_gla_gb[0]])[None, :]
    gn = g_gla_norm[0][None, :]
    win = _bf(w_gla_in[0])
    og_p, new_f, new_b = _gla_call(xp, mod[0], win, w1, w2, b2, gn, None, None, prompt=True)
    (og_s,) = _gla_call(xs, mod[0], win, w1, w2, b2, gn, state_gla_fwd[:, 0], state_gla_bwd[:, 0],
                        prompt=False)
    x1, h2, aff = _post_mixer(og_p, og_s, _bf(w_gla_out[0]), (xp, xs), mod[0], ln_g[0, 0][None, :],
                              ln_b[0, 0][None, :], w_router[0].T)
    x_all = _moe(h2, aff, x1, mod[0], ln_g[0, 1][None, :], ln_b[0, 1][None, :],
                 w_moe_gate, w_moe_up, w_moe_down, 0, False)

    cos_t, sin_t = _rope_tables()
    q_all, k_all, v_all, new_k, new_v = _qkv(x_all, mod[1], _bf(w_attn_in[0]), g_attn_q[0][None, :],
                                             g_attn_k[0][None, :], cos_t, sin_t)
    oa_p, oa_s = _attention(q_all, k_all, v_all, cache_attn_k[:, 0], cache_attn_v[:, 0])
    x1, h2, aff = _post_mixer(oa_p, oa_s, _bf(w_attn_out[0]), (x_all,), mod[1], ln_g[1, 0][None, :],
                              ln_b[1, 0][None, :], w_router[1].T)
    yp, ys = _moe(h2, aff, x1, mod[1], ln_g[1, 1][None, :], ln_b[1, 1][None, :],
                  w_moe_gate, w_moe_up, w_moe_down, 1, True)
    return (yp.reshape(BATCH, SEQ, D), ys.reshape(DEC_BATCH, DEC_SEQ, D), new_f, new_b, new_k, new_v)
```
